```python
import math, functools
import jax, jax.numpy as jnp
from jax import lax
import numpy as np

D_MODEL = 2048
BATCH = 2
SEQ = 4096
DEPTH = 1
DEC_BATCH = 128
DEC_SEQ = 4
PAST_LEN = 8192
PAGE_SIZE = 128

MLA_HEADS = 8
QK_NOPE = 128
QK_ROPE = 64
V_HEAD = 128
KV_LORA = 512
ROPE_THETA = 10000.0
Q_BLOCK = 128
MLA_WIDTH = MLA_HEADS * V_HEAD
Q_DIM = MLA_HEADS * (QK_NOPE + QK_ROPE)
SSM_D_INNER = D_MODEL // 2
SSM_HEADDIM = 64
SSM_HEADS = SSM_D_INNER // SSM_HEADDIM
SSM_GROUPS = 2
SSM_STATE = 128
CONV_WIDTH = 4
CHUNK_SIZE = 128
CONV_DIM = SSM_D_INNER + 2 * SSM_GROUPS * SSM_STATE
MIX_WIDTH = MLA_WIDTH + SSM_D_INNER
IN_DIM = Q_DIM + KV_LORA + QK_ROPE + SSM_D_INNER + CONV_DIM + SSM_HEADS
D_FF = 5632
HALF_STEP = 0.5
PLE_DIM = 256
EPS = 1e-6

kernel_name = 'hymba_mla_ssd_macaron_step'


def rmsnorm(x, g):
    xf = x.astype(jnp.float32)
    y = xf * lax.rsqrt(jnp.mean(xf * xf, axis=-1, keepdims=True) + EPS)
    return (y * g.astype(jnp.float32)).astype(x.dtype)


def swiglu(h, w1, w3, w2):
    return (jax.nn.silu(h @ w1) * (h @ w3)) @ w2


def rope(x, pos):
    half = x.shape[-1] // 2
    inv = ROPE_THETA ** (-jnp.arange(half, dtype=jnp.float32) / half)
    ang = pos.astype(jnp.float32)[:, None] * inv[None, :]
    ang = ang.reshape((ang.shape[0],) + (1,) * (x.ndim - 3) + (half,))
    cos, sin = jnp.cos(ang).astype(x.dtype), jnp.sin(ang).astype(x.dtype)
    x1, x2 = x[..., :half], x[..., half:]
    return jnp.concatenate([x1 * cos - x2 * sin, x1 * sin + x2 * cos], axis=-1)


def split_in(proj):
    sizes = [Q_DIM, KV_LORA, QK_ROPE, SSM_D_INNER, CONV_DIM, SSM_HEADS]
    return jnp.split(proj, np.cumsum(sizes)[:-1].tolist(), axis=-1)


def mla_prompt(q_nope, q_rope, ckv, k_rope, w_uk, w_uv):
    b, s = ckv.shape[:2]
    k_nope = jnp.einsum('bsc,chd->bshd', ckv, w_uk)
    v = jnp.einsum('bsc,chd->bshd', ckv, w_uv)
    scale = (QK_NOPE + QK_ROPE) ** -0.5
    nblk = s // Q_BLOCK
    kpos = jnp.arange(s)

    def block(args):
        qn, qr, start = args
        sc = (jnp.einsum('bqhd,bkhd->bhqk', qn, k_nope)
              + jnp.einsum('bqhr,bkr->bhqk', qr, k_rope)).astype(jnp.float32) * scale
        qpos = start + jnp.arange(Q_BLOCK)
        sc = jnp.where(kpos[None, :] <= qpos[:, None], sc, -jnp.inf)
        p = jax.nn.softmax(sc, axis=-1).astype(v.dtype)
        return jnp.einsum('bhqk,bkhd->bqhd', p, v)

    qn_b = q_nope.reshape(b, nblk, Q_BLOCK, MLA_HEADS, QK_NOPE).swapaxes(0, 1)
    qr_b = q_rope.reshape(b, nblk, Q_BLOCK, MLA_HEADS, QK_ROPE).swapaxes(0, 1)
    starts = jnp.arange(nblk) * Q_BLOCK
    out = lax.map(block, (qn_b, qr_b, starts))
    return out.swapaxes(0, 1).reshape(b, s, MLA_WIDTH)


def mla_sample(q_nope, q_rope, ckv, k_rope, w_uk, w_uv, ckv_past, kr_past):
    t = ckv.shape[1]
    past_len = ckv_past.shape[1]
    scale = (QK_NOPE + QK_ROPE) ** -0.5
    q_lat = jnp.einsum('bthd,chd->bthc', q_nope, w_uk)
    s_past = (jnp.einsum('bthc,bsc->bhts', q_lat, ckv_past)
              + jnp.einsum('bthr,bsr->bhts', q_rope, kr_past)).astype(jnp.float32) * scale
    s_new = (jnp.einsum('bthc,bsc->bhts', q_lat, ckv)
             + jnp.einsum('bthr,bsr->bhts', q_rope, k_rope)).astype(jnp.float32) * scale
    s_new = jnp.where(jnp.tril(jnp.ones((t, t), bool)), s_new, -jnp.inf)
    p = jax.nn.softmax(jnp.concatenate([s_past, s_new], axis=-1), axis=-1).astype(ckv.dtype)
    o_lat = (jnp.einsum('bhts,bsc->bthc', p[..., :past_len], ckv_past)
             + jnp.einsum('bhts,bsc->bthc', p[..., past_len:], ckv))
    out = jnp.einsum('bthc,chd->bthd', o_lat, w_uv)
    return out.reshape(out.shape[0], t, MLA_WIDTH)


def causal_conv(xbc, conv_prev, w, bias):
    L = xbc.shape[1]
    xp = jnp.concatenate([conv_prev.astype(xbc.dtype), xbc], axis=1)
    y = bias
    for k in range(CONV_WIDTH):
        y = y + xp[:, k:k + L] * w[k]
    return jax.nn.silu(y), xp[:, xp.shape[1] - (CONV_WIDTH - 1):]


def segsum(a):
    T = a.shape[-1]
    cs = jnp.cumsum(a, axis=-1)
    diff = cs[..., :, None] - cs[..., None, :]
    return jnp.where(jnp.tril(jnp.ones((T, T), bool)), diff, -jnp.inf)


def ssd(x, dt, A, B, C, init_state):
    b, L, h, p = x.shape
    n = B.shape[-1]
    cl = math.gcd(L, CHUNK_SIZE)
    nc = L // cl
    xdt = (x.astype(jnp.float32) * dt[..., None]).reshape(b, nc, cl, h, p)
    Bc = B.astype(jnp.float32).reshape(b, nc, cl, h, n)
    Cc = C.astype(jnp.float32).reshape(b, nc, cl, h, n)
    a = (dt * A).reshape(b, nc, cl, h).transpose(0, 3, 1, 2)
    a_cs = jnp.cumsum(a, axis=-1)
    scores = jnp.einsum('bclhn,bcshn->bhcls', Cc, Bc) * jnp.exp(segsum(a))
    y_diag = jnp.einsum('bhcls,bcshp->bclhp', scores, xdt)
    decay_states = jnp.exp(a_cs[..., -1:] - a_cs)
    states = jnp.einsum('bclhn,bhcl,bclhp->bchpn', Bc, decay_states, xdt)
    states = jnp.concatenate([init_state.astype(jnp.float32)[:, None], states], axis=1)
    chunk_decay = jnp.exp(segsum(jnp.pad(a_cs[..., -1], ((0, 0), (0, 0), (1, 0)))))
    states = jnp.einsum('bhzc,bchpn->bzhpn', chunk_decay, states)
    prev_states, final = states[:, :-1], states[:, -1]
    y_off = jnp.einsum('bclhn,bchpn,bhcl->bclhp', Cc, prev_states, jnp.exp(a_cs))
    y = (y_diag + y_off).reshape(b, L, h, p)
    return y.astype(x.dtype), final.astype(x.dtype)


def mamba_mixer(z, xbc_raw, dt_raw, conv_prev, ssm_prev, conv_w, conv_b, dt_bias, a_log, d_skip, g_ssm):
    b, L = z.shape[:2]
    xbc, conv_new = causal_conv(xbc_raw, conv_prev, conv_w, conv_b)
    xs, Bs, Cs = jnp.split(xbc, [SSM_D_INNER, SSM_D_INNER + SSM_GROUPS * SSM_STATE], axis=-1)
    xs = xs.reshape(b, L, SSM_HEADS, SSM_HEADDIM)
    rep = SSM_HEADS // SSM_GROUPS
    Bs = jnp.repeat(Bs.reshape(b, L, SSM_GROUPS, SSM_STATE), rep, axis=2)
    Cs = jnp.repeat(Cs.reshape(b, L, SSM_GROUPS, SSM_STATE), rep, axis=2)
    dt = jax.nn.softplus(dt_raw.astype(jnp.float32) + dt_bias.astype(jnp.float32))
    A = -jnp.exp(a_log.astype(jnp.float32))
    y, ssm_new = ssd(xs, dt, A, Bs, Cs, ssm_prev)
    y = (y + xs * d_skip[:, None]).reshape(b, L, SSM_D_INNER) * jax.nn.silu(z)
    y = rmsnorm(y.reshape(b, L, SSM_GROUPS, SSM_D_INNER // SSM_GROUPS),
                g_ssm.reshape(SSM_GROUPS, -1)).reshape(b, L, SSM_D_INNER)
    return y, conv_new, ssm_new


def trunk_layer(x, pe, pos, conv_prev, ssm_prev, attn, lw):
    (g_ffn1, w1_a, w3_a, w2_a, g_mix, w_in, g_kv, w_uk, w_uv, conv_w, conv_b,
     dt_bias, a_log, d_skip, g_ssm, w_out, g_ffn2, w1_b, w3_b, w2_b,
     g_ple, w_ple_gate, w_ple_proj) = lw
    b, L = x.shape[:2]
    x = x + HALF_STEP * swiglu(rmsnorm(x, g_ffn1), w1_a, w3_a, w2_a)
    h = rmsnorm(x, g_mix)
    q, ckv_raw, kr_raw, z, xbc_raw, dt_raw = split_in(h @ w_in)
    q = q.reshape(b, L, MLA_HEADS, QK_NOPE + QK_ROPE)
    q_nope, q_rope = q[..., :QK_NOPE], rope(q[..., QK_NOPE:], pos)
    ckv = rmsnorm(ckv_raw, g_kv)
    k_rope = rope(kr_raw, pos)
    o_att = attn(q_nope, q_rope, ckv, k_rope, w_uk, w_uv)
    o_ssm, conv_new, ssm_new = mamba_mixer(z, xbc_raw, dt_raw, conv_prev, ssm_prev, conv_w, conv_b,
                                           dt_bias, a_log, d_skip, g_ssm)
    x = x + jnp.concatenate([o_att, o_ssm], axis=-1) @ w_out
    x = x + HALF_STEP * swiglu(rmsnorm(x, g_ffn2), w1_b, w3_b, w2_b)
    x = x + jax.nn.sigmoid(rmsnorm(x, g_ple) @ w_ple_gate) * (pe @ w_ple_proj)
    return x, ckv, k_rope, conv_new, ssm_new


def _normal(k, shape, scale):
    return jax.random.normal(k, shape, jnp.float32) * scale


def setup_inputs(seed: int = 0) -> dict:
    key = jax.random.key(seed)
    ks = list(jax.random.split(key, 40))
    n_pages = PAST_LEN // PAGE_SIZE
    n_used = DEC_BATCH * n_pages
    n_phys = max((n_used * 5) // 4, n_used + 1)
    page_table = jax.random.permutation(ks[0], n_phys)[:n_used].reshape(DEC_BATCH, n_pages).astype(jnp.int32)
    dt0 = jnp.exp(jax.random.uniform(ks[1], (DEPTH, SSM_HEADS), jnp.float32, math.log(1e-3), math.log(1e-1)))
    gain = lambda k, d: 1.0 + _normal(k, (DEPTH, d), 0.02)
    return {
        'x_prompt': _normal(ks[2], (BATCH, SEQ, D_MODEL), 1.0),
        'x_sample': _normal(ks[3], (DEC_BATCH, DEC_SEQ, D_MODEL), 1.0),
        'p_prompt': _normal(ks[4], (DEPTH, BATCH, SEQ, PLE_DIM), 1.0),
        'p_sample': _normal(ks[5], (DEPTH, DEC_BATCH, DEC_SEQ, PLE_DIM), 1.0),
        'cache_ckv': _normal(ks[6], (DEPTH, n_phys, PAGE_SIZE, KV_LORA), 1.0),
        'cache_krope': _normal(ks[7], (DEPTH, n_phys, PAGE_SIZE, QK_ROPE), 1.0),
        'state_conv': _normal(ks[8], (DEPTH, DEC_BATCH, CONV_WIDTH - 1, CONV_DIM), 1.0),
        'state_ssm': _normal(ks[9], (DEPTH, DEC_BATCH, SSM_HEADS, SSM_HEADDIM, SSM_STATE), 0.5),
        'page_table': page_table,
        'g_ffn1': gain(ks[10], D_MODEL),
        'w1_a': _normal(ks[11], (DEPTH, D_MODEL, D_FF), D_MODEL ** -0.5),
        'w3_a': _normal(ks[12], (DEPTH, D_MODEL, D_FF), D_MODEL ** -0.5),
        'w2_a': _normal(ks[13], (DEPTH, D_FF, D_MODEL), D_FF ** -0.5),
        'g_mix': gain(ks[14], D_MODEL),
        'w_in': _normal(ks[15], (DEPTH, D_MODEL, IN_DIM), D_MODEL ** -0.5),
        'g_kv': gain(ks[16], KV_LORA),
        'w_uk': _normal(ks[17], (DEPTH, KV_LORA, MLA_HEADS, QK_NOPE), KV_LORA ** -0.5),
        'w_uv': _normal(ks[18], (DEPTH, KV_LORA, MLA_HEADS, V_HEAD), KV_LORA ** -0.5),
        'conv_w': _normal(ks[19], (DEPTH, CONV_WIDTH, CONV_DIM), CONV_WIDTH ** -0.5),
        'conv_b': _normal(ks[20], (DEPTH, CONV_DIM), 0.02),
        'dt_bias': dt0 + jnp.log(-jnp.expm1(-dt0)),
        'a_log': jnp.log(jax.random.uniform(ks[21], (DEPTH, SSM_HEADS), jnp.float32, 1.0, 16.0)),
        'd_skip': 1.0 + _normal(ks[22], (DEPTH, SSM_HEADS), 0.02),
        'g_ssm': gain(ks[23], SSM_D_INNER),
        'w_out': _normal(ks[24], (DEPTH, MIX_WIDTH, D_MODEL), MIX_WIDTH ** -0.5),
        'g_ffn2': gain(ks[25], D_MODEL),
        'w1_b': _normal(ks[26], (DEPTH, D_MODEL, D_FF), D_MODEL ** -0.5),
        'w3_b': _normal(ks[27], (DEPTH, D_MODEL, D_FF), D_MODEL ** -0.5),
        'w2_b': _normal(ks[28], (DEPTH, D_FF, D_MODEL), D_FF ** -0.5),
        'g_ple': gain(ks[29], D_MODEL),
        'w_ple_gate': _normal(ks[30], (DEPTH, D_MODEL, D_MODEL), D_MODEL ** -0.5),
        'w_ple_proj': _normal(ks[31], (DEPTH, PLE_DIM, D_MODEL), PLE_DIM ** -0.5),
        'g_final': 1.0 + _normal(ks[32], (D_MODEL,), 0.02),
    }


def reference(x_prompt, x_sample, p_prompt, p_sample, cache_ckv, cache_krope, state_conv, state_ssm,
              page_table, g_ffn1, w1_a, w3_a, w2_a, g_mix, w_in, g_kv, w_uk, w_uv, conv_w, conv_b,
              dt_bias, a_log, d_skip, g_ssm, w_out, g_ffn2, w1_b, w3_b, w2_b, g_ple, w_ple_gate,
              w_ple_proj, g_final):
    b_p, s_p = x_prompt.shape[:2]
    b_s, t_s = x_sample.shape[:2]
    past_len = page_table.shape[1] * cache_ckv.shape[2]
    pos_p = jnp.arange(s_p)
    pos_s = past_len + jnp.arange(t_s)
    hp, hs = x_prompt, x_sample
    ckv_p, kr_p, conv_p, ssm_p = [], [], [], []
    ckv_s, kr_s, conv_s, ssm_s = [], [], [], []
    for i in range(DEPTH):
        lw = (g_ffn1[i], w1_a[i], w3_a[i], w2_a[i], g_mix[i], w_in[i], g_kv[i], w_uk[i], w_uv[i],
              conv_w[i], conv_b[i], dt_bias[i], a_log[i], d_skip[i], g_ssm[i], w_out[i], g_ffn2[i],
              w1_b[i], w3_b[i], w2_b[i], g_ple[i], w_ple_gate[i], w_ple_proj[i])
        conv0 = jnp.zeros((b_p, CONV_WIDTH - 1, CONV_DIM), hp.dtype)
        ssm0 = jnp.zeros((b_p, SSM_HEADS, SSM_HEADDIM, SSM_STATE), hp.dtype)
        hp, c1, k1, cv1, s1 = trunk_layer(hp, p_prompt[i], pos_p, conv0, ssm0, mla_prompt, lw)
        ckv_past = cache_ckv[i][page_table].reshape(b_s, past_len, KV_LORA)
        kr_past = cache_krope[i][page_table].reshape(b_s, past_len, QK_ROPE)
        attn_s = functools.partial(mla_sample, ckv_past=ckv_past, kr_past=kr_past)
        hs, c2, k2, cv2, s2 = trunk_layer(hs, p_sample[i], pos_s, state_conv[i], state_ssm[i], attn_s, lw)
        ckv_p.append(c1); kr_p.append(k1); conv_p.append(cv1); ssm_p.append(s1)
        ckv_s.append(c2); kr_s.append(k2); conv_s.append(cv2); ssm_s.append(s2)
    y_prompt = rmsnorm(hp, g_final)
    y_sample = rmsnorm(hs, g_final)
    return (y_prompt, y_sample,
            jnp.stack(ckv_p), jnp.stack(kr_p), jnp.stack(conv_p), jnp.stack(ssm_p),
            jnp.stack(ckv_s), jnp.stack(kr_s), jnp.stack(conv_s), jnp.stack(ssm_s))
```

```python
import functools
import math

import jax
import jax.numpy as jnp
from jax import lax
from jax.experimental import pallas as pl
from jax.experimental.pallas import tpu as pltpu

F32 = jnp.float32
BF16 = jnp.bfloat16

EPS = 1e-6
HALF_STEP = 0.5
ROPE_THETA = 10000.0
SSD_CHUNK = 128
CONV_WIDTH = 4
LANES = 128
BF16_ROWS = 16
CONV_PAD = 8
V7X_VMEM_LIMIT = 56 * 1024 * 1024

FFN_ROWS = 512
FFN_COLS = 512
MIX_ROWS = 256
ROW_TILE = 512
ATTN_Q = 512
ATTN_K = 512
DECODE_PAGES = 8

_NT = (((1,), (1,)), ((), ()))
_TN = (((0,), (0,)), ((), ()))


def _params(*sem):
    return pltpu.CompilerParams(dimension_semantics=sem, vmem_limit_bytes=V7X_VMEM_LIMIT)


def _rms(x, g):
    return x * lax.rsqrt(jnp.mean(x * x, axis=-1, keepdims=True) + EPS) * g


def _dot(a, b):
    return jnp.dot(a, b, preferred_element_type=F32)


def _split3(v):
    hi = v.astype(BF16)
    r1 = v - hi.astype(F32)
    mid = r1.astype(BF16)
    lo = (r1 - mid.astype(F32)).astype(BF16)
    return hi, mid, lo


def _exact_dot(a, b, dims=None, split="rhs"):
    f = (lambda x, y: _dot(x, y)) if dims is None else (lambda x, y: lax.dot_general(x, y, dims, preferred_element_type=F32))
    if split == "rhs":
        h, m, l = _split3(b)
        return f(a, h) + f(a, m) + f(a, l)
    h, m, l = _split3(a)
    return f(h, b) + f(m, b) + f(l, b)


def _ffn_body(x_ref, g_ref, w1_ref, w3_ref, w2_ref, o_ref, xn_ref, acc_ref):
    j = pl.program_id(1)

    @pl.when(j == 0)
    def _():
        xn_ref[...] = _rms(x_ref[...], g_ref[...]).astype(BF16)
        acc_ref[...] = jnp.zeros_like(acc_ref)

    xn = xn_ref[...]
    h1 = _dot(xn, w1_ref[...])
    h3 = _dot(xn, w3_ref[...])
    act = (h1 * jax.nn.sigmoid(h1) * h3).astype(BF16)
    acc_ref[...] += _dot(act, w2_ref[...])

    @pl.when(j == pl.num_programs(1) - 1)
    def _():
        o_ref[...] = x_ref[...] + HALF_STEP * acc_ref[...]


def _ffn(x, g, w1, w3, w2):
    n, d = x.shape
    f = w1.shape[1]
    tm, tf = min(FFN_ROWS, n), FFN_COLS
    return pl.pallas_call(
        _ffn_body,
        out_shape=jax.ShapeDtypeStruct((n, d), F32),
        grid=(n // tm, f // tf),
        in_specs=[
            pl.BlockSpec((tm, d), lambda i, j: (i, 0)),
            pl.BlockSpec((1, d), lambda i, j: (0, 0)),
            pl.BlockSpec((d, tf), lambda i, j: (0, j)),
            pl.BlockSpec((d, tf), lambda i, j: (0, j)),
            pl.BlockSpec((tf, d), lambda i, j: (j, 0)),
        ],
        out_specs=pl.BlockSpec((tm, d), lambda i, j: (i, 0)),
        scratch_shapes=[pltpu.VMEM((tm, d), BF16), pltpu.VMEM((tm, d), F32)],
        compiler_params=_params("parallel", "arbitrary"),
        name="ffn_half_step",
    )(x, g, w1, w3, w2)


def _mix_attn_body(x_ref, g_ref, cos_ref, sin_ref, wqn_ref, wqr_ref, wqrr_ref, wckv_ref, wkr_ref, wkrr_ref,
                   gkv_ref, *rest, heads, nope, rope, scale, prompt):
    wa_ref = rest[0]
    wb_ref, outs = (rest[1], rest[2:]) if prompt else (None, rest[1:])
    h = _rms(x_ref[...], g_ref[...]).astype(BF16)
    cos = cos_ref[...]
    sin = sin_ref[...]
    qn = _dot(h, wqn_ref[...]) * scale
    qr = (_dot(h, wqr_ref[...]) * cos + _dot(h, wqrr_ref[...]) * sin) * scale
    ckv = _rms(_dot(h, wckv_ref[...]), gkv_ref[...])
    kr = _dot(h, wkr_ref[...]) * cos[:, :rope] + _dot(h, wkrr_ref[...]) * sin[:, :rope]
    if prompt:
        q3_ref, k3_ref, v3_ref, ckv_ref, kr_ref = outs
        ckv_b = ckv.astype(BF16)
        kn = _dot(ckv_b, wa_ref[...])
        v = _dot(ckv_b, wb_ref[...])
        kr_b = kr.astype(BF16)
        vd = v.shape[1] // heads
        for hd in range(heads):
            q3_ref[hd, :, 0:nope] = qn[:, hd * nope:(hd + 1) * nope].astype(BF16)
            q3_ref[hd, :, nope:nope + rope] = qr[:, hd * rope:(hd + 1) * rope].astype(BF16)
            k3_ref[hd, :, 0:nope] = kn[:, hd * nope:(hd + 1) * nope].astype(BF16)
            k3_ref[hd, :, nope:nope + rope] = kr_b
            v3_ref[hd] = v[:, hd * vd:(hd + 1) * vd].astype(BF16)
    else:
        ql_ref, qr_ref, ckv_ref, kr_ref = outs
        lat = wa_ref.shape[2]
        for hd in range(heads):
            qh = qn[:, hd * nope:(hd + 1) * nope].astype(BF16)
            ql_ref[:, hd * lat:(hd + 1) * lat] = _dot(qh, wa_ref[hd]).astype(BF16)
        qr_ref[...] = qr.astype(BF16)
    ckv_ref[...] = ckv
    kr_ref[...] = kr


def _mix_attn(x, g, cos, sin, w, *, prompt, scale):
    n, d = x.shape
    heads, nope, rope, lat = w["heads"], w["nope"], w["rope"], w["lat"]
    tm = min(MIX_ROWS, n)
    period = cos.shape[0] // tm
    const = lambda i: (0, 0)
    row = lambda i: (i, 0)
    if prompt:
        ws = [w["w_uk"], w["w_uv"]]
        vd = ws[1].shape[1] // heads
        wspecs = [pl.BlockSpec(ws[0].shape, const), pl.BlockSpec(ws[1].shape, const)]
        out_shape = [jax.ShapeDtypeStruct((heads, n, nope + rope), BF16),
                     jax.ShapeDtypeStruct((heads, n, nope + rope), BF16),
                     jax.ShapeDtypeStruct((heads, n, vd), BF16)]
        out_specs = [pl.BlockSpec((heads, tm, nope + rope), lambda i: (0, i, 0)),
                     pl.BlockSpec((heads, tm, nope + rope), lambda i: (0, i, 0)),
                     pl.BlockSpec((heads, tm, vd), lambda i: (0, i, 0))]
    else:
        ws = [w["w_uk_t"]]
        wspecs = [pl.BlockSpec(ws[0].shape, lambda i: (0, 0, 0))]
        out_shape = [jax.ShapeDtypeStruct((n, heads * lat), BF16),
                     jax.ShapeDtypeStruct((n, heads * rope), BF16)]
        out_specs = [pl.BlockSpec((tm, heads * lat), row), pl.BlockSpec((tm, heads * rope), row)]
    out_shape += [jax.ShapeDtypeStruct((n, lat), F32), jax.ShapeDtypeStruct((n, rope), F32)]
    out_specs += [pl.BlockSpec((tm, lat), row), pl.BlockSpec((tm, rope), row)]
    body = functools.partial(_mix_attn_body, heads=heads, nope=nope, rope=rope, scale=scale, prompt=prompt)
    return pl.pallas_call(
        body,
        out_shape=out_shape,
        grid=(n // tm,),
        in_specs=[
            pl.BlockSpec((tm, d), row),
            pl.BlockSpec((1, d), const),
            pl.BlockSpec((tm, heads * rope), lambda i: (i % period, 0)),
            pl.BlockSpec((tm, heads * rope), lambda i: (i % period, 0)),
            pl.BlockSpec(w["w_qn"].shape, const),
            pl.BlockSpec(w["w_qr"].shape, const),
            pl.BlockSpec(w["w_qr_rot"].shape, const),
            pl.BlockSpec(w["w_ckv"].shape, const),
            pl.BlockSpec(w["w_kr"].shape, const),
            pl.BlockSpec(w["w_kr_rot"].shape, const),
            pl.BlockSpec((1, lat), const),
        ] + wspecs,
        out_specs=out_specs,
        compiler_params=_params("parallel"),
        name="mix_attn_prompt" if prompt else "mix_attn_sample",
    )(x, g, cos, sin, w["w_qn"], w["w_qr"], w["w_qr_rot"], w["w_ckv"], w["w_kr"], w["w_kr_rot"], w["g_kv"], *ws)


def _mix_ssm_body(x_ref, g_ref, wz_ref, wx_ref, wdt_ref, z_ref, xbc_ref, dt_ref):
    h = _rms(x_ref[...], g_ref[...]).astype(BF16)
    z_ref[...] = _dot(h, wz_ref[...])
    xbc_ref[...] = _dot(h, wx_ref[...])
    dt_ref[...] = _dot(h, wdt_ref[...])


def _mix_ssm(x, g, w):
    n, d = x.shape
    tm = min(MIX_ROWS, n)
    const = lambda i: (0, 0)
    row = lambda i: (i, 0)
    ws = [w["w_z"], w["w_xbc"], w["w_dt"]]
    return pl.pallas_call(
        _mix_ssm_body,
        out_shape=[jax.ShapeDtypeStruct((n, wi.shape[1]), F32) for wi in ws],
        grid=(n // tm,),
        in_specs=[pl.BlockSpec((tm, d), row), pl.BlockSpec((1, d), const)] + [pl.BlockSpec(wi.shape, const) for wi in ws],
        out_specs=[pl.BlockSpec((tm, wi.shape[1]), row) for wi in ws],
        compiler_params=_params("parallel"),
        name="mix_ssm",
    )(x, g, *ws)


def _softmax_step(q, kb, vb, carry, mask):
    m, l, acc = carry
    s = lax.dot_general(q, kb, _NT, preferred_element_type=F32)
    if mask is not None:
        s = jnp.where(mask, s, -jnp.inf)
    m_new = jnp.maximum(m, jnp.max(s, axis=-1, keepdims=True))
    alpha = jnp.exp(m - m_new)
    p = jnp.exp(s - m_new)
    l = alpha * l + jnp.sum(p, axis=-1, keepdims=True)
    acc = alpha * acc + _dot(p.astype(BF16), vb)
    return m_new, l, acc


def _attn_body(q_ref, k_ref, v_ref, o_ref, *, tq, tk):
    i = pl.program_id(2)
    q = q_ref[0]
    vd = v_ref.shape[2]

    def full_block(j, carry):
        off = pl.multiple_of(j * tk, tk)
        return _softmax_step(q, k_ref[0, pl.ds(off, tk), :], v_ref[0, pl.ds(off, tk), :], carry, None)

    carry = (jnp.full((tq, 1), -jnp.inf, F32), jnp.zeros((tq, 1), F32), jnp.zeros((tq, vd), F32))
    carry = lax.fori_loop(0, i * (tq // tk), full_block, carry)
    rows = lax.broadcasted_iota(jnp.int32, (tq, tk), 0)
    cols = lax.broadcasted_iota(jnp.int32, (tq, tk), 1)
    for dblk in range(tq // tk):
        off = pl.multiple_of(i * tq + dblk * tk, tk)
        carry = _softmax_step(q, k_ref[0, pl.ds(off, tk), :], v_ref[0, pl.ds(off, tk), :], carry,
                              cols + dblk * tk <= rows)
    _, l, acc = carry
    o_ref[...] = (acc / l).astype(BF16)


def _attn_prompt(q3, k3, v3, batch):
    heads, n, dq = q3.shape
    vd = v3.shape[2]
    s = n // batch
    tq = min(ATTN_Q, s)
    tk = min(ATTN_K, tq)
    nq = s // tq
    return pl.pallas_call(
        functools.partial(_attn_body, tq=tq, tk=tk),
        out_shape=jax.ShapeDtypeStruct((n, heads * vd), BF16),
        grid=(batch, heads, nq),
        in_specs=[
            pl.BlockSpec((1, tq, dq), lambda b, h, i: (h, b * nq + i, 0)),
            pl.BlockSpec((1, s, dq), lambda b, h, i: (h, b, 0)),
            pl.BlockSpec((1, s, vd), lambda b, h, i: (h, b, 0)),
        ],
        out_specs=pl.BlockSpec((tq, vd), lambda b, h, i: (b * nq + i, h)),
        compiler_params=_params("parallel", "parallel", "arbitrary"),
        name="attn_prompt",
    )(q3, k3, v3)


def _decode_body(pt_ref, ql_ref, qr_ref, cn_ref, kn_ref, *rest, pages, heads):
    ckv_refs = rest[:pages]
    kr_refs = rest[pages:2 * pages]
    o_ref, m_ref, l_ref, acc_ref = rest[2 * pages:]
    p = pl.program_id(1)

    @pl.when(p == 0)
    def _():
        m_ref[...] = jnp.full_like(m_ref, -jnp.inf)
        l_ref[...] = jnp.zeros_like(l_ref)
        acc_ref[...] = jnp.zeros_like(acc_ref)

    ql = ql_ref[0]
    qr = qr_ref[0]

    def update(kv, kr, mask):
        s = lax.dot_general(ql, kv, _NT, preferred_element_type=F32) + lax.dot_general(qr, kr, _NT, preferred_element_type=F32)
        if mask is not None:
            s = jnp.where(mask, s, -jnp.inf)
        m = m_ref[...]
        m_new = jnp.maximum(m, jnp.max(s, axis=-1, keepdims=True))
        alpha = jnp.exp(m - m_new)
        pr = jnp.exp(s - m_new)
        l_ref[...] = alpha * l_ref[...] + jnp.sum(pr, axis=-1, keepdims=True)
        acc_ref[...] = alpha * acc_ref[...] + _dot(pr.astype(BF16), kv)
        m_ref[...] = m_new

    kv = jnp.concatenate([r[0, 0] for r in ckv_refs], axis=0).astype(BF16)
    kr = jnp.concatenate([r[0, 0] for r in kr_refs], axis=0).astype(BF16)
    update(kv, kr, None)

    @pl.when(p == pl.num_programs(1) - 1)
    def _():
        rows, nk = ql.shape[0], cn_ref.shape[1]
        qtok = lax.broadcasted_iota(jnp.int32, (rows, nk), 0) // heads
        ktok = lax.broadcasted_iota(jnp.int32, (rows, nk), 1)
        update(cn_ref[0], kn_ref[0], ktok <= qtok)
        o_ref[0] = (acc_ref[...] / l_ref[...]).astype(BF16)


def _attn_decode(layer, page_table, ql, qr, ckv_new, kr_new, cache_ckv, cache_krope, heads):
    bsz, rows, lat = ql.shape
    rope = qr.shape[2]
    n_pages = page_table.shape[1]
    page = cache_ckv.shape[2]
    g = DECODE_PAGES if n_pages % DECODE_PAGES == 0 else 1
    nk = ckv_new.shape[1]
    bmap = lambda b, p, pt: (b, 0, 0)

    def page_map(k):
        return lambda b, p, pt: (layer, pt[b, p * g + k], 0, 0)

    grid_spec = pltpu.PrefetchScalarGridSpec(
        num_scalar_prefetch=1,
        grid=(bsz, n_pages // g),
        in_specs=[
            pl.BlockSpec((1, rows, lat), bmap),
            pl.BlockSpec((1, rows, rope), bmap),
            pl.BlockSpec((1, nk, lat), bmap),
            pl.BlockSpec((1, nk, rope), bmap),
        ] + [pl.BlockSpec((1, 1, page, lat), page_map(k)) for k in range(g)]
          + [pl.BlockSpec((1, 1, page, rope), page_map(k)) for k in range(g)],
        out_specs=pl.BlockSpec((1, rows, lat), bmap),
        scratch_shapes=[pltpu.VMEM((rows, 1), F32), pltpu.VMEM((rows, 1), F32), pltpu.VMEM((rows, lat), F32)],
    )
    return pl.pallas_call(
        functools.partial(_decode_body, pages=g, heads=heads),
        grid_spec=grid_spec,
        out_shape=jax.ShapeDtypeStruct((bsz, rows, lat), BF16),
        compiler_params=_params("parallel", "arbitrary"),
        name="attn_decode",
    )(page_table, ql, qr, ckv_new, kr_new, *([cache_ckv] * g), *([cache_krope] * g))


def _uv_body(ol_ref, wuv_ref, o_ref, *, heads):
    lat, vd = wuv_ref.shape[1], wuv_ref.shape[2]
    for hd in range(heads):
        o_ref[:, hd * vd:(hd + 1) * vd] = _dot(ol_ref[:, hd * lat:(hd + 1) * lat], wuv_ref[hd]).astype(BF16)


def _uv_proj(ol, wuv_h):
    n = ol.shape[0]
    heads, lat, vd = wuv_h.shape
    tm = min(ROW_TILE, n)
    return pl.pallas_call(
        functools.partial(_uv_body, heads=heads),
        out_shape=jax.ShapeDtypeStruct((n, heads * vd), BF16),
        grid=(n // tm,),
        in_specs=[pl.BlockSpec((tm, heads * lat), lambda i: (i, 0)), pl.BlockSpec(wuv_h.shape, lambda i: (0, 0, 0))],
        out_specs=pl.BlockSpec((tm, heads * vd), lambda i: (i, 0)),
        compiler_params=_params("parallel"),
        name="uv_proj",
    )(ol, wuv_h)


def _ssd_body(xbc_ref, z_ref, dt_ref, cprev_ref, sprev_ref, cw_ref, cb_ref, dtb_ref, alog_ref, dsk_ref, g_ref, r_ref,
              y_ref, cnew_ref, snew_ref, state_ref, xp_ref, *, chunk, l_real, heads, headdim, groups, d_state):
    c = pl.program_id(1)
    last = pl.num_programs(1) - 1
    d_inner = heads * headdim
    gw = d_inner // groups
    hpg = heads // groups
    sp = max(chunk, LANES)

    @pl.when(c == 0)
    def _():
        state_ref[...] = sprev_ref[0]
        xp_ref[0:CONV_PAD, :] = cprev_ref[0]

    xp_ref[CONV_PAD:CONV_PAD + chunk, :] = xbc_ref[0]
    cw = cw_ref[...]
    y = cb_ref[...]
    for k in range(CONV_WIDTH):
        y = y + xp_ref[CONV_PAD - (CONV_WIDTH - 1) + k:CONV_PAD - (CONV_WIDTH - 1) + k + chunk, :] * cw[k:k + 1, :]
    xbc = y * jax.nn.sigmoid(y)
    tail = xp_ref[l_real:l_real + CONV_PAD, :]
    xp_ref[0:CONV_PAD, :] = tail

    @pl.when(c == last)
    def _():
        cnew_ref[0] = tail

    xs = xbc[:, :d_inner]
    bm = xbc[:, d_inner:d_inner + groups * d_state].astype(BF16)
    cm = xbc[:, d_inner + groups * d_state:].astype(BF16)

    t_row = lax.broadcasted_iota(jnp.int32, (chunk, LANES), 0)
    dtr = dt_ref[0] + dtb_ref[...]
    dt = jnp.maximum(dtr, 0.0) + jnp.log1p(jnp.exp(-jnp.abs(dtr)))
    if l_real < chunk:
        dt = jnp.where(t_row < l_real, dt, 0.0)
    a = dt * (-jnp.exp(alog_ref[...]))
    tri = (lax.broadcasted_iota(jnp.int32, (chunk, chunk), 0) >= lax.broadcasted_iota(jnp.int32, (chunk, chunk), 1))
    cs = _exact_dot(jnp.where(tri, 1.0, 0.0).astype(BF16), a)
    cs_last = cs[chunk - 1:chunk, :]
    ecs = jnp.exp(cs)
    dec = jnp.exp(cs_last - cs)
    expanded = _exact_dot(jnp.concatenate([dt, ecs, dec], axis=0), r_ref[...], split="lhs")
    dt_e = expanded[0:chunk]
    ecs_e = expanded[chunk:2 * chunk]
    dec_e = expanded[2 * chunk:3 * chunk]
    xdt = xs * dt_e

    def pad_time(v):
        if sp == chunk:
            return v
        return jnp.concatenate([v, jnp.zeros((sp - chunk, v.shape[1]), v.dtype)], axis=0)

    eye = (lax.broadcasted_iota(jnp.int32, (LANES, LANES), 0) == lax.broadcasted_iota(jnp.int32, (LANES, LANES), 1))
    cs_t = _exact_dot(jnp.where(eye, 1.0, 0.0).astype(BF16), pad_time(cs), dims=_NT)
    xdt_p = pad_time(xdt.astype(BF16))
    bm_p = pad_time(bm)
    causal = lax.broadcasted_iota(jnp.int32, (chunk, sp), 1) <= lax.broadcasted_iota(jnp.int32, (chunk, sp), 0)
    ecl = jnp.exp(cs_last)

    y_groups = []
    for g in range(groups):
        cg = cm[:, g * d_state:(g + 1) * d_state]
        cb_mat = lax.dot_general(cg, bm_p[:, g * d_state:(g + 1) * d_state], _NT, preferred_element_type=F32)
        sg = state_ref[g * gw:(g + 1) * gw, :]
        y_off = lax.dot_general(cg, sg.astype(BF16), _NT, preferred_element_type=F32)
        y_heads = []
        for hh in range(hpg):
            hd = g * hpg + hh
            seg = jnp.exp(jnp.where(causal, cs[:, hd:hd + 1] - cs_t[hd:hd + 1, :], -jnp.inf))
            y_heads.append(_dot((cb_mat * seg).astype(BF16), xdt_p[:, hd * headdim:(hd + 1) * headdim]))
        y_groups.append(jnp.concatenate(y_heads, axis=1) + y_off * ecs_e[:, g * gw:(g + 1) * gw])
        xd = (xdt[:, g * gw:(g + 1) * gw] * dec_e[:, g * gw:(g + 1) * gw]).astype(BF16)
        upd = lax.dot_general(xd, bm[:, g * d_state:(g + 1) * d_state], _TN, preferred_element_type=F32)
        for hh in range(hpg):
            hd = g * hpg + hh
            r0 = g * gw + hh * headdim
            state_ref[r0:r0 + headdim, :] = (sg[hh * headdim:(hh + 1) * headdim, :] * ecl[0:1, hd:hd + 1]
                                             + upd[hh * headdim:(hh + 1) * headdim, :])

    yv = jnp.concatenate(y_groups, axis=1) + xs * dsk_ref[...]
    zv = z_ref[0]
    yv = yv * (zv * jax.nn.sigmoid(zv))
    gv = g_ref[...]
    outs = [_rms(yv[:, g * gw:(g + 1) * gw], gv[:, g * gw:(g + 1) * gw]) for g in range(groups)]
    y_ref[0] = jnp.concatenate(outs, axis=1).astype(BF16)

    @pl.when(c == last)
    def _():
        snew_ref[0] = state_ref[...]


def _ssd(xbc, z, dt, conv_prev, state_prev, w, *, chunk, l_real):
    bsz, t, cd = xbc.shape
    heads, headdim, groups, d_state = w["ssm_heads"], w["ssm_headdim"], w["ssm_groups"], w["ssm_state"]
    d_inner = heads * headdim
    nc = t // chunk
    tok = lambda b, c: (b, c, 0)
    seq = lambda b, c: (b, 0, 0)
    const = lambda b, c: (0, 0)
    consts = [w["conv_w"], w["conv_b"], w["dt_bias"], w["a_log"], w["d_skip"], w["g_ssm"], w["head_expand"]]
    body = functools.partial(_ssd_body, chunk=chunk, l_real=l_real, heads=heads, headdim=headdim, groups=groups,
                             d_state=d_state)
    return pl.pallas_call(
        body,
        out_shape=[jax.ShapeDtypeStruct((bsz, t, d_inner), BF16),
                   jax.ShapeDtypeStruct((bsz, CONV_PAD, cd), F32),
                   jax.ShapeDtypeStruct((bsz, d_inner, d_state), F32)],
        grid=(bsz, nc),
        in_specs=[
            pl.BlockSpec((1, chunk, cd), tok),
            pl.BlockSpec((1, chunk, d_inner), tok),
            pl.BlockSpec((1, chunk, LANES), tok),
            pl.BlockSpec((1, CONV_PAD, cd), seq),
            pl.BlockSpec((1, d_inner, d_state), seq),
        ] + [pl.BlockSpec(v.shape, const) for v in consts],
        out_specs=[pl.BlockSpec((1, chunk, d_inner), tok),
                   pl.BlockSpec((1, CONV_PAD, cd), seq),
                   pl.BlockSpec((1, d_inner, d_state), seq)],
        scratch_shapes=[pltpu.VMEM((d_inner, d_state), F32), pltpu.VMEM((CONV_PAD + chunk, cd), F32)],
        compiler_params=_params("parallel", "arbitrary"),
        name="ssd_mixer",
    )(xbc, z, dt, conv_prev, state_prev, *consts)


def _out_proj_body(x_ref, oa_ref, os_ref, wa_ref, ws_ref, o_ref):
    o_ref[...] = x_ref[...] + _dot(oa_ref[...], wa_ref[...]) + _dot(os_ref[...], ws_ref[...])


def _out_proj(x, oa, osm, wa, ws):
    n, d = x.shape
    tm = min(ROW_TILE, n)
    row = lambda i: (i, 0)
    const = lambda i: (0, 0)
    return pl.pallas_call(
        _out_proj_body,
        out_shape=jax.ShapeDtypeStruct((n, d), F32),
        grid=(n // tm,),
        in_specs=[pl.BlockSpec((tm, d), row), pl.BlockSpec((tm, oa.shape[1]), row), pl.BlockSpec((tm, osm.shape[1]), row),
                  pl.BlockSpec(wa.shape, const), pl.BlockSpec(ws.shape, const)],
        out_specs=pl.BlockSpec((tm, d), row),
        compiler_params=_params("parallel"),
        name="out_proj",
    )(x, oa, osm, wa, ws)


def _ple_body(x_ref, pe_ref, g_ref, wg_ref, wp_ref, gf_ref, o_ref, *, final):
    x = x_ref[...]
    gate = jax.nn.sigmoid(_dot(_rms(x, g_ref[...]).astype(BF16), wg_ref[...]))
    x = x + gate * _dot(pe_ref[...].astype(BF16), wp_ref[...])
    o_ref[...] = _rms(x, gf_ref[...]) if final else x


def _ple(x, pe, g, wg, wp, g_final, final):
    n, d = x.shape
    tm = min(ROW_TILE, n)
    row = lambda i: (i, 0)
    const = lambda i: (0, 0)
    return pl.pallas_call(
        functools.partial(_ple_body, final=final),
        out_shape=jax.ShapeDtypeStruct((n, d), F32),
        grid=(n // tm,),
        in_specs=[pl.BlockSpec((tm, d), row), pl.BlockSpec((tm, pe.shape[1]), row), pl.BlockSpec((1, d), const),
                  pl.BlockSpec(wg.shape, const), pl.BlockSpec(wp.shape, const), pl.BlockSpec((1, d), const)],
        out_specs=pl.BlockSpec((tm, d), row),
        compiler_params=_params("parallel"),
        name="ple_final",
    )(x, pe, g, wg, wp, g_final)


def _rope_tables(pos, rope, heads):
    half = rope // 2
    inv = ROPE_THETA ** (-jnp.arange(half, dtype=F32) / half)
    ang = pos.astype(F32)[:, None] * inv[None, :]
    cos, sin = jnp.cos(ang), jnp.sin(ang)
    tile = lambda t: jnp.tile(jnp.concatenate([t, t], axis=-1), (1, heads))
    return tile(cos), tile(sin)


def _rot_cols(w, rope):
    d = w.shape[0]
    wr = w.reshape(d, -1, 2, rope // 2)
    return jnp.stack([-wr[:, :, 1], wr[:, :, 0]], axis=2).reshape(d, -1)


def _prep_layer(i, p):
    d = p["w_in"].shape[1]
    lat, heads, nope = p["w_uk"].shape[1:]
    vd = p["w_uv"].shape[3]
    rope = p["cache_krope"].shape[3]
    ssm_heads, headdim, d_state = p["state_ssm"].shape[2:]
    d_inner = ssm_heads * headdim
    conv_dim = p["conv_w"].shape[2]
    groups = (conv_dim - d_inner) // (2 * d_state)
    q_dim = heads * (nope + rope)
    w_in = p["w_in"][i]
    o = 0
    wq = w_in[:, o:o + q_dim].reshape(d, heads, nope + rope); o += q_dim
    w_ckv = w_in[:, o:o + lat]; o += lat
    w_kr = w_in[:, o:o + rope]; o += rope
    w_z = w_in[:, o:o + d_inner]; o += d_inner
    w_xbc = w_in[:, o:o + conv_dim]; o += conv_dim
    w_dt = w_in[:, o:o + ssm_heads]
    w_qr = wq[:, :, nope:].reshape(d, heads * rope)
    b = lambda t: t.astype(BF16)
    pad_heads = lambda v: jnp.pad(v.reshape(1, -1), ((0, 0), (0, LANES - ssm_heads)))
    head_of_channel = jnp.arange(d_inner) // headdim
    w_out = p["w_out"][i]
    return dict(
        heads=heads, nope=nope, rope=rope, lat=lat,
        ssm_heads=ssm_heads, ssm_headdim=headdim, ssm_groups=groups, ssm_state=d_state,
        g_ffn1=p["g_ffn1"][i][None], w1_a=b(p["w1_a"][i]), w3_a=b(p["w3_a"][i]), w2_a=b(p["w2_a"][i]),
        g_ffn2=p["g_ffn2"][i][None], w1_b=b(p["w1_b"][i]), w3_b=b(p["w3_b"][i]), w2_b=b(p["w2_b"][i]),
        g_mix=p["g_mix"][i][None],
        w_qn=b(wq[:, :, :nope].reshape(d, heads * nope)), w_qr=b(w_qr), w_qr_rot=b(_rot_cols(w_qr, rope)),
        w_ckv=b(w_ckv), w_kr=b(w_kr), w_kr_rot=b(_rot_cols(w_kr, rope)), g_kv=p["g_kv"][i][None],
        w_uk=b(p["w_uk"][i].reshape(lat, heads * nope)), w_uv=b(p["w_uv"][i].reshape(lat, heads * vd)),
        w_uk_t=b(jnp.transpose(p["w_uk"][i], (1, 2, 0))), w_uv_h=b(jnp.transpose(p["w_uv"][i], (1, 0, 2))),
        w_z=b(w_z), w_xbc=b(w_xbc), w_dt=b(jnp.pad(w_dt, ((0, 0), (0, LANES - ssm_heads)))),
        conv_w=p["conv_w"][i], conv_b=p["conv_b"][i][None], dt_bias=pad_heads(p["dt_bias"][i]),
        a_log=pad_heads(p["a_log"][i]), d_skip=jnp.repeat(p["d_skip"][i], headdim)[None], g_ssm=p["g_ssm"][i][None],
        head_expand=(jnp.arange(LANES)[:, None] == head_of_channel[None, :]).astype(BF16),
        w_out_a=b(w_out[:heads * vd]), w_out_s=b(w_out[heads * vd:]),
        g_ple=p["g_ple"][i][None], w_ple_gate=b(p["w_ple_gate"][i]), w_ple_proj=b(p["w_ple_proj"][i]),
    )


def _trunk_layer(x, pe, cos, sin, conv_prev, state_prev, attn, w, g_final, *, bsz, chunk, l_real, final):
    n, d = x.shape
    t = n // bsz
    scale = (w["nope"] + w["rope"]) ** -0.5
    x = _ffn(x, w["g_ffn1"], w["w1_a"], w["w3_a"], w["w2_a"])
    o_att, ckv, kr = attn(x, cos, sin, scale)
    z, xbc, dt = _mix_ssm(x, w["g_mix"], w)
    seq = lambda v: v.reshape(bsz, t, v.shape[1])
    if t % chunk:
        pad = lambda v: jnp.pad(seq(v), ((0, 0), (0, chunk - t), (0, 0)))
    else:
        pad = seq
    o_ssm, conv_new, ssm_new = _ssd(pad(xbc), pad(z), pad(dt), conv_prev, state_prev, w, chunk=chunk, l_real=l_real)
    o_ssm = o_ssm[:, :t].reshape(n, -1)
    x = _out_proj(x, o_att, o_ssm, w["w_out_a"], w["w_out_s"])
    x = _ffn(x, w["g_ffn2"], w["w1_b"], w["w3_b"], w["w2_b"])
    x = _ple(x, pe, w["g_ple"], w["w_ple_gate"], w["w_ple_proj"], g_final, final)
    return x, ckv, kr, conv_new[:, CONV_PAD - (CONV_WIDTH - 1):], ssm_new


def kernel(x_prompt, x_sample, p_prompt, p_sample, cache_ckv, cache_krope, state_conv, state_ssm, page_table, g_ffn1, w1_a, w3_a, w2_a, g_mix, w_in, g_kv, w_uk, w_uv, conv_w, conv_b, dt_bias, a_log, d_skip, g_ssm, w_out, g_ffn2, w1_b, w3_b, w2_b, g_ple, w_ple_gate, w_ple_proj, g_final):
    params = dict(cache_krope=cache_krope, state_ssm=state_ssm, g_ffn1=g_ffn1, w1_a=w1_a, w3_a=w3_a, w2_a=w2_a,
                  g_mix=g_mix, w_in=w_in, g_kv=g_kv, w_uk=w_uk, w_uv=w_uv, conv_w=conv_w, conv_b=conv_b,
                  dt_bias=dt_bias, a_log=a_log, d_skip=d_skip, g_ssm=g_ssm, w_out=w_out, g_ffn2=g_ffn2, w1_b=w1_b,
                  w3_b=w3_b, w2_b=w2_b, g_ple=g_ple, w_ple_gate=w_ple_gate, w_ple_proj=w_ple_proj)
    depth = w_in.shape[0]
    b_p, s_p, d = x_prompt.shape
    b_s, t_s, _ = x_sample.shape
    heads, rope = w_uk.shape[2], cache_krope.shape[3]
    lat = w_uk.shape[1]
    ssm_heads, headdim, d_state = state_ssm.shape[2:]
    d_inner = ssm_heads * headdim
    conv_dim = conv_w.shape[2]
    past_len = page_table.shape[1] * cache_ckv.shape[2]
    cos_p, sin_p = _rope_tables(jnp.arange(s_p), rope, heads)
    cos_s, sin_s = _rope_tables(past_len + jnp.arange(t_s), rope, heads)
    cos_s, sin_s = jnp.tile(cos_s, (b_s, 1)), jnp.tile(sin_s, (b_s, 1))
    chunk_p = math.gcd(s_p, SSD_CHUNK)
    chunk_s = -(-t_s // BF16_ROWS) * BF16_ROWS
    g_fin = g_final[None]
    hp = x_prompt.reshape(b_p * s_p, d)
    hs = x_sample.reshape(b_s * t_s, d)
    outs = [[] for _ in range(8)]
    for i in range(depth):
        w = _prep_layer(i, params)
        final = i == depth - 1

        def attn_p(x, cos, sin, scale):
            q3, k3, v3, ckv, kr = _mix_attn(x, w["g_mix"], cos, sin, w, prompt=True, scale=scale)
            return _attn_prompt(q3, k3, v3, b_p), ckv, kr

        def attn_s(x, cos, sin, scale):
            ql, qr, ckv, kr = _mix_attn(x, w["g_mix"], cos, sin, w, prompt=False, scale=scale)
            new = lambda v: jnp.pad(v.reshape(b_s, t_s, -1), ((0, 0), (0, chunk_s - t_s), (0, 0))).astype(BF16)
            ol = _attn_decode(i, page_table, ql.reshape(b_s, t_s * heads, lat), qr.reshape(b_s, t_s * heads, rope),
                              new(ckv), new(kr), cache_ckv, cache_krope, heads)
            return _uv_proj(ol.reshape(b_s * t_s, heads * lat), w["w_uv_h"]), ckv, kr

        conv0 = jnp.zeros((b_p, CONV_PAD, conv_dim), F32)
        ssm0 = jnp.zeros((b_p, d_inner, d_state), F32)
        hp, c1, k1, cv1, s1 = _trunk_layer(hp, p_prompt[i].reshape(b_p * s_p, -1), cos_p, sin_p, conv0, ssm0, attn_p, w,
                                           g_fin, bsz=b_p, chunk=chunk_p, l_real=chunk_p, final=final)
        conv_s = jnp.pad(state_conv[i], ((0, 0), (CONV_PAD - (CONV_WIDTH - 1), 0), (0, 0)))
        hs, c2, k2, cv2, s2 = _trunk_layer(hs, p_sample[i].reshape(b_s * t_s, -1), cos_s, sin_s, conv_s,
                                           state_ssm[i].reshape(b_s, d_inner, d_state), attn_s, w, g_fin,
                                           bsz=b_s, chunk=chunk_s, l_real=t_s, final=final)
        for lst, v in zip(outs, (c1.reshape(b_p, s_p, -1), k1.reshape(b_p, s_p, -1), cv1,
                                 s1.reshape(b_p, ssm_heads, headdim, d_state),
                                 c2.reshape(b_s, t_s, -1), k2.reshape(b_s, t_s, -1), cv2,
                                 s2.reshape(b_s, ssm_heads, headdim, d_state))):
            lst.append(v)
    return (hp.reshape(b_p, s_p, d), hs.reshape(b_s, t_s, d)) + tuple(jnp.stack(o) for o in outs)
```

```python
import functools
import math

import jax
import jax.numpy as jnp
from jax import lax
from jax.experimental import pallas as pl
from jax.experimental.pallas import tpu as pltpu

F32 = jnp.float32
BF16 = jnp.bfloat16

EPS = 1e-6
HALF_STEP = 0.5
ROPE_THETA = 10000.0
SSD_CHUNK = 128
CONV_WIDTH = 4
LANES = 128
BF16_ROWS = 16
CONV_PAD = 8
V7X_VMEM_LIMIT = 56 * 1024 * 1024

FFN_ROWS = 1024
FFN_COLS = 256
MIX_ROWS = 256
ROW_TILE = 512
ATTN_Q = 512
ATTN_K = 512
ATTN_HEADS = 4
LOG2_E = math.log2(math.e)
DECODE_PAGES = 32
SSD_SEQS = 4

_NT = (((1,), (1,)), ((), ()))
_TN = (((0,), (0,)), ((), ()))


def _params(*sem):
    return pltpu.CompilerParams(dimension_semantics=sem, vmem_limit_bytes=V7X_VMEM_LIMIT)


def _rms(x, g):
    return x * lax.rsqrt(jnp.mean(x * x, axis=-1, keepdims=True) + EPS) * g


def _dot(a, b):
    return jnp.dot(a, b, preferred_element_type=F32)


def _split3(v):
    hi = v.astype(BF16)
    r1 = v - hi.astype(F32)
    mid = r1.astype(BF16)
    lo = (r1 - mid.astype(F32)).astype(BF16)
    return hi, mid, lo


def _exact_dot(a, b, dims=None, split="rhs"):
    f = (lambda x, y: _dot(x, y)) if dims is None else (lambda x, y: lax.dot_general(x, y, dims, preferred_element_type=F32))
    if split == "rhs":
        h, m, l = _split3(b)
        return f(a, h) + f(a, m) + f(a, l)
    h, m, l = _split3(a)
    return f(h, b) + f(m, b) + f(l, b)


def _ffn_body(x_ref, g_ref, w1_ref, w3_ref, w2_ref, o_ref, xn_ref):
    @pl.when(pl.program_id(1) == 0)
    def _():
        x = x_ref[...]
        xn_ref[...] = _rms(x, g_ref[...]).astype(BF16)
        o_ref[...] = x

    xn = xn_ref[...]
    h1 = _dot(xn, w1_ref[...].astype(BF16))
    h3 = _dot(xn, w3_ref[...].astype(BF16))
    act = (h1 * jax.nn.sigmoid(h1) * h3).astype(BF16)
    o_ref[...] += HALF_STEP * _dot(act, w2_ref[...].astype(BF16))


def _ffn(x, g, w1, w3, w2):
    n, d = x.shape
    f = w1.shape[1]
    tm, tf = min(FFN_ROWS, n), FFN_COLS
    return pl.pallas_call(
        _ffn_body,
        out_shape=jax.ShapeDtypeStruct((n, d), F32),
        grid=(n // tm, f // tf),
        in_specs=[
            pl.BlockSpec((tm, d), lambda i, j: (i, 0), pipeline_mode=pl.Buffered(1)),
            pl.BlockSpec((1, d), lambda i, j: (0, 0)),
            pl.BlockSpec((d, tf), lambda i, j: (0, j)),
            pl.BlockSpec((d, tf), lambda i, j: (0, j)),
            pl.BlockSpec((tf, d), lambda i, j: (j, 0)),
        ],
        out_specs=pl.BlockSpec((tm, d), lambda i, j: (i, 0)),
        scratch_shapes=[pltpu.VMEM((tm, d), BF16)],
        compiler_params=_params("parallel", "arbitrary"),
        name="ffn_half_step",
    )(x, g, w1, w3, w2)


def _mix_attn_body(x_ref, g_ref, cos_ref, sin_ref, wqn_ref, wqr_ref, wqrr_ref, wckv_ref, wkr_ref, wkrr_ref,
                   gkv_ref, *rest, heads, nope, rope, scale, prompt):
    wa_ref = rest[0]
    wb_ref, outs = (rest[1], rest[2:]) if prompt else (None, rest[1:])
    h = _rms(x_ref[...], g_ref[...]).astype(BF16)
    cos = cos_ref[...]
    sin = sin_ref[...]
    qn = _dot(h, wqn_ref[...]) * scale
    qr = (_dot(h, wqr_ref[...]) * cos + _dot(h, wqrr_ref[...]) * sin) * scale
    ckv = _rms(_dot(h, wckv_ref[...]), gkv_ref[...])
    kr = _dot(h, wkr_ref[...]) * cos[:, :rope] + _dot(h, wkrr_ref[...]) * sin[:, :rope]
    if prompt:
        q3_ref, k3_ref, v3_ref, ckv_ref, kr_ref = outs
        ckv_b = ckv.astype(BF16)
        kn = _dot(ckv_b, wa_ref[...])
        v = _dot(ckv_b, wb_ref[...])
        kr_b = kr.astype(BF16)
        vd = v.shape[1] // heads
        for hd in range(heads):
            q3_ref[hd, :, 0:nope] = qn[:, hd * nope:(hd + 1) * nope].astype(BF16)
            q3_ref[hd, :, nope:nope + rope] = qr[:, hd * rope:(hd + 1) * rope].astype(BF16)
            k3_ref[hd, :, 0:nope] = kn[:, hd * nope:(hd + 1) * nope].astype(BF16)
            k3_ref[hd, :, nope:nope + rope] = kr_b
            v3_ref[hd] = v[:, hd * vd:(hd + 1) * vd].astype(BF16)
    else:
        ql_ref, qr_ref, ckv_ref, kr_ref = outs
        lat = wa_ref.shape[2]
        for hd in range(heads):
            qh = qn[:, hd * nope:(hd + 1) * nope].astype(BF16)
            ql_ref[:, hd * lat:(hd + 1) * lat] = _dot(qh, wa_ref[hd]).astype(BF16)
        qr_ref[...] = qr.astype(BF16)
    ckv_ref[...] = ckv
    kr_ref[...] = kr


def _mix_attn(x, g, cos, sin, w, *, prompt, scale):
    n, d = x.shape
    heads, nope, rope, lat = w["heads"], w["nope"], w["rope"], w["lat"]
    tm = min(MIX_ROWS, n)
    period = cos.shape[0] // tm
    const = lambda i: (0, 0)
    row = lambda i: (i, 0)
    if prompt:
        ws = [w["w_uk"], w["w_uv"]]
        vd = ws[1].shape[1] // heads
        wspecs = [pl.BlockSpec(ws[0].shape, const), pl.BlockSpec(ws[1].shape, const)]
        out_shape = [jax.ShapeDtypeStruct((heads, n, nope + rope), BF16),
                     jax.ShapeDtypeStruct((heads, n, nope + rope), BF16),
                     jax.ShapeDtypeStruct((heads, n, vd), BF16)]
        out_specs = [pl.BlockSpec((heads, tm, nope + rope), lambda i: (0, i, 0)),
                     pl.BlockSpec((heads, tm, nope + rope), lambda i: (0, i, 0)),
                     pl.BlockSpec((heads, tm, vd), lambda i: (0, i, 0))]
    else:
        ws = [w["w_uk_t"]]
        wspecs = [pl.BlockSpec(ws[0].shape, lambda i: (0, 0, 0))]
        out_shape = [jax.ShapeDtypeStruct((n, heads * lat), BF16),
                     jax.ShapeDtypeStruct((n, heads * rope), BF16)]
        out_specs = [pl.BlockSpec((tm, heads * lat), row), pl.BlockSpec((tm, heads * rope), row)]
    out_shape += [jax.ShapeDtypeStruct((n, lat), F32), jax.ShapeDtypeStruct((n, rope), F32)]
    out_specs += [pl.BlockSpec((tm, lat), row), pl.BlockSpec((tm, rope), row)]
    body = functools.partial(_mix_attn_body, heads=heads, nope=nope, rope=rope, scale=scale, prompt=prompt)
    return pl.pallas_call(
        body,
        out_shape=out_shape,
        grid=(n // tm,),
        in_specs=[
            pl.BlockSpec((tm, d), row),
            pl.BlockSpec((1, d), const),
            pl.BlockSpec((tm, heads * rope), lambda i: (i % period, 0)),
            pl.BlockSpec((tm, heads * rope), lambda i: (i % period, 0)),
            pl.BlockSpec(w["w_qn"].shape, const),
            pl.BlockSpec(w["w_qr"].shape, const),
            pl.BlockSpec(w["w_qr_rot"].shape, const),
            pl.BlockSpec(w["w_ckv"].shape, const),
            pl.BlockSpec(w["w_kr"].shape, const),
            pl.BlockSpec(w["w_kr_rot"].shape, const),
            pl.BlockSpec((1, lat), const),
        ] + wspecs,
        out_specs=out_specs,
        compiler_params=_params("parallel"),
        name="mix_attn_prompt" if prompt else "mix_attn_sample",
    )(x, g, cos, sin, w["w_qn"], w["w_qr"], w["w_qr_rot"], w["w_ckv"], w["w_kr"], w["w_kr_rot"], w["g_kv"], *ws)


def _mix_ssm_body(x_ref, g_ref, wz_ref, wx_ref, wdt_ref, z_ref, xbc_ref, dt_ref):
    h = _rms(x_ref[...], g_ref[...]).astype(BF16)
    z_ref[...] = _dot(h, wz_ref[...])
    xbc_ref[...] = _dot(h, wx_ref[...])
    dt_ref[...] = _dot(h, wdt_ref[...])


def _mix_ssm(x, g, w):
    n, d = x.shape
    tm = min(MIX_ROWS, n)
    const = lambda i: (0, 0)
    row = lambda i: (i, 0)
    ws = [w["w_z"], w["w_xbc"], w["w_dt"]]
    return pl.pallas_call(
        _mix_ssm_body,
        out_shape=[jax.ShapeDtypeStruct((n, wi.shape[1]), F32) for wi in ws],
        grid=(n // tm,),
        in_specs=[pl.BlockSpec((tm, d), row), pl.BlockSpec((1, d), const)] + [pl.BlockSpec(wi.shape, const) for wi in ws],
        out_specs=[pl.BlockSpec((tm, wi.shape[1]), row) for wi in ws],
        compiler_params=_params("parallel"),
        name="mix_ssm",
    )(x, g, *ws)


def _softmax_step(q, kb, vb, carry, mask):
    m, l, acc = carry
    s = lax.dot_general(q, kb, _NT, preferred_element_type=F32)
    if mask is not None:
        s = jnp.where(mask, s, -jnp.inf)
    m_new = jnp.maximum(m, jnp.max(s, axis=-1, keepdims=True))
    alpha = jnp.exp2(m - m_new)
    p = jnp.exp2(s - m_new)
    l = alpha * l + jnp.sum(p, axis=-1, keepdims=True)
    acc = alpha * acc + _dot(p.astype(BF16), vb)
    return m_new, l, acc


def _attn_body(q_ref, k_ref, v_ref, o_ref, *, tq, tk, nh):
    i = pl.program_id(2)
    vd = v_ref.shape[2]
    qs = [q_ref[h] for h in range(nh)]

    def blocks(off, carries, mask):
        return tuple(_softmax_step(qs[h], k_ref[h, pl.ds(off, tk), :], v_ref[h, pl.ds(off, tk), :], carries[h], mask)
                     for h in range(nh))

    def full_block(j, carries):
        return blocks(pl.multiple_of(j * tk, tk), carries, None)

    carries = tuple((jnp.full((tq, 1), -jnp.inf, F32), jnp.zeros((tq, 1), F32), jnp.zeros((tq, vd), F32))
                    for _ in range(nh))
    carries = lax.fori_loop(0, i * (tq // tk), full_block, carries)
    rows = lax.broadcasted_iota(jnp.int32, (tq, tk), 0)
    cols = lax.broadcasted_iota(jnp.int32, (tq, tk), 1)
    for dblk in range(tq // tk):
        carries = blocks(pl.multiple_of(i * tq + dblk * tk, tk), carries, cols + dblk * tk <= rows)
    for h, (_, l, acc) in enumerate(carries):
        o_ref[:, h * vd:(h + 1) * vd] = (acc / l).astype(BF16)


def _attn_prompt(q3, k3, v3, batch):
    heads, n, dq = q3.shape
    vd = v3.shape[2]
    s = n // batch
    tq = min(ATTN_Q, s)
    tk = min(ATTN_K, tq)
    nq = s // tq
    nh = math.gcd(ATTN_HEADS, heads)
    return pl.pallas_call(
        functools.partial(_attn_body, tq=tq, tk=tk, nh=nh),
        out_shape=jax.ShapeDtypeStruct((n, heads * vd), BF16),
        grid=(batch, heads // nh, nq),
        in_specs=[
            pl.BlockSpec((nh, tq, dq), lambda b, h, i: (h, b * nq + i, 0)),
            pl.BlockSpec((nh, s, dq), lambda b, h, i: (h, b, 0)),
            pl.BlockSpec((nh, s, vd), lambda b, h, i: (h, b, 0)),
        ],
        out_specs=pl.BlockSpec((tq, nh * vd), lambda b, h, i: (b * nq + i, h)),
        compiler_params=_params("parallel", "parallel", "arbitrary"),
        name="attn_prompt",
    )(q3, k3, v3)


def _decode_body(pt_ref, ql_ref, qr_ref, cn_ref, kn_ref, *rest, pages, heads):
    ckv_refs = rest[:pages]
    kr_refs = rest[pages:2 * pages]
    o_ref, m_ref, l_ref, acc_ref = rest[2 * pages:]
    p = pl.program_id(1)

    @pl.when(p == 0)
    def _():
        m_ref[...] = jnp.full_like(m_ref, -jnp.inf)
        l_ref[...] = jnp.zeros_like(l_ref)
        acc_ref[...] = jnp.zeros_like(acc_ref)

    ql = ql_ref[0]
    qr = qr_ref[0]

    def update(s, kv):
        m = m_ref[...]
        m_new = jnp.maximum(m, jnp.max(s, axis=-1, keepdims=True))
        alpha = jnp.exp(m - m_new)
        pr = jnp.exp(s - m_new)
        l_ref[...] = alpha * l_ref[...] + jnp.sum(pr, axis=-1, keepdims=True)
        acc_ref[...] = alpha * acc_ref[...] + _dot(pr.astype(BF16), kv)
        m_ref[...] = m_new

    kv = jnp.concatenate([r[0, 0] for r in ckv_refs], axis=0).astype(BF16)
    kr_t = jnp.concatenate([r[0, 0] for r in kr_refs], axis=1).astype(BF16)
    update(lax.dot_general(ql, kv, _NT, preferred_element_type=F32) + _dot(qr, kr_t), kv)

    @pl.when(p == pl.num_programs(1) - 1)
    def _():
        rows, nk = ql.shape[0], cn_ref.shape[1]
        qtok = lax.broadcasted_iota(jnp.int32, (rows, nk), 0) // heads
        ktok = lax.broadcasted_iota(jnp.int32, (rows, nk), 1)
        kvn = cn_ref[0]
        s = (lax.dot_general(ql, kvn, _NT, preferred_element_type=F32)
             + lax.dot_general(qr, kn_ref[0], _NT, preferred_element_type=F32))
        update(jnp.where(ktok <= qtok, s, -jnp.inf), kvn)
        o_ref[0] = (acc_ref[...] / l_ref[...]).astype(BF16)


def _attn_decode(layer, page_table, ql, qr, ckv_new, kr_new, cache_ckv, cache_krope_t, heads):
    bsz, rows, lat = ql.shape
    rope = qr.shape[2]
    n_pages = page_table.shape[1]
    page = cache_ckv.shape[2]
    g = math.gcd(DECODE_PAGES, n_pages)
    nk = ckv_new.shape[1]
    bmap = lambda b, p, pt: (b, 0, 0)

    def page_map(k):
        return lambda b, p, pt: (layer, pt[b, p * g + k], 0, 0)

    grid_spec = pltpu.PrefetchScalarGridSpec(
        num_scalar_prefetch=1,
        grid=(bsz, n_pages // g),
        in_specs=[
            pl.BlockSpec((1, rows, lat), bmap),
            pl.BlockSpec((1, rows, rope), bmap),
            pl.BlockSpec((1, nk, lat), bmap),
            pl.BlockSpec((1, nk, rope), bmap),
        ] + [pl.BlockSpec((1, 1, page, lat), page_map(k)) for k in range(g)]
          + [pl.BlockSpec((1, 1, rope, page), page_map(k)) for k in range(g)],
        out_specs=pl.BlockSpec((1, rows, lat), bmap),
        scratch_shapes=[pltpu.VMEM((rows, 1), F32), pltpu.VMEM((rows, 1), F32), pltpu.VMEM((rows, lat), F32)],
    )
    return pl.pallas_call(
        functools.partial(_decode_body, pages=g, heads=heads),
        grid_spec=grid_spec,
        out_shape=jax.ShapeDtypeStruct((bsz, rows, lat), BF16),
        compiler_params=_params("parallel", "arbitrary"),
        name="attn_decode",
    )(page_table, ql, qr, ckv_new, kr_new, *([cache_ckv] * g), *([cache_krope_t] * g))


def _uv_body(ol_ref, wuv_ref, o_ref, *, heads):
    lat, vd = wuv_ref.shape[1], wuv_ref.shape[2]
    for hd in range(heads):
        o_ref[:, hd * vd:(hd + 1) * vd] = _dot(ol_ref[:, hd * lat:(hd + 1) * lat], wuv_ref[hd]).astype(BF16)


def _uv_proj(ol, wuv_h):
    n = ol.shape[0]
    heads, lat, vd = wuv_h.shape
    tm = min(ROW_TILE, n)
    return pl.pallas_call(
        functools.partial(_uv_body, heads=heads),
        out_shape=jax.ShapeDtypeStruct((n, heads * vd), BF16),
        grid=(n // tm,),
        in_specs=[pl.BlockSpec((tm, heads * lat), lambda i: (i, 0)), pl.BlockSpec(wuv_h.shape, lambda i: (0, 0, 0))],
        out_specs=pl.BlockSpec((tm, heads * vd), lambda i: (i, 0)),
        compiler_params=_params("parallel"),
        name="uv_proj",
    )(ol, wuv_h)


def _ssd_body(xbc_ref, z_ref, dt_ref, cprev_ref, sprev_ref, cw_ref, cb_ref, dtb_ref, alog_ref, dsk_ref, g_ref, r_ref,
              y_ref, cnew_ref, snew_ref, state_ref, xp_ref, *, nseq, **dims):
    for sq in range(nseq):
        _ssd_seq(sq, xbc_ref, z_ref, dt_ref, cprev_ref, sprev_ref, cw_ref, cb_ref, dtb_ref, alog_ref, dsk_ref, g_ref,
                 r_ref, y_ref, cnew_ref, snew_ref, state_ref, xp_ref, **dims)


def _ssd_seq(sq, xbc_ref, z_ref, dt_ref, cprev_ref, sprev_ref, cw_ref, cb_ref, dtb_ref, alog_ref, dsk_ref, g_ref, r_ref,
             y_ref, cnew_ref, snew_ref, state_ref, xp_ref, *, chunk, l_real, heads, headdim, groups, d_state):
    c = pl.program_id(1)
    last = pl.num_programs(1) - 1
    d_inner = heads * headdim
    gw = d_inner // groups
    hpg = heads // groups
    sp = max(chunk, LANES)

    @pl.when(c == 0)
    def _():
        state_ref[sq] = sprev_ref[sq]
        xp_ref[sq, 0:CONV_PAD, :] = cprev_ref[sq]

    xp_ref[sq, CONV_PAD:CONV_PAD + chunk, :] = xbc_ref[sq]
    cw = cw_ref[...]
    y = cb_ref[...]
    for k in range(CONV_WIDTH):
        y = y + xp_ref[sq, CONV_PAD - (CONV_WIDTH - 1) + k:CONV_PAD - (CONV_WIDTH - 1) + k + chunk, :] * cw[k:k + 1, :]
    xbc = y * jax.nn.sigmoid(y)
    tail = xp_ref[sq, l_real:l_real + CONV_PAD, :]
    xp_ref[sq, 0:CONV_PAD, :] = tail

    @pl.when(c == last)
    def _():
        cnew_ref[sq] = tail

    xs = xbc[:, :d_inner]
    bm = xbc[:, d_inner:d_inner + groups * d_state].astype(BF16)
    cm = xbc[:, d_inner + groups * d_state:].astype(BF16)

    t_row = lax.broadcasted_iota(jnp.int32, (chunk, LANES), 0)
    dtr = dt_ref[sq] + dtb_ref[...]
    dt = jnp.maximum(dtr, 0.0) + jnp.log1p(jnp.exp(-jnp.abs(dtr)))
    if l_real < chunk:
        dt = jnp.where(t_row < l_real, dt, 0.0)
    a = dt * (-jnp.exp(alog_ref[...]))
    tri = (lax.broadcasted_iota(jnp.int32, (chunk, chunk), 0) >= lax.broadcasted_iota(jnp.int32, (chunk, chunk), 1))
    cs = _exact_dot(jnp.where(tri, 1.0, 0.0).astype(BF16), a)
    cs_last = cs[chunk - 1:chunk, :]
    ecs = jnp.exp(cs)
    dec = jnp.exp(cs_last - cs)
    expanded = _exact_dot(jnp.concatenate([dt, ecs, dec], axis=0), r_ref[...], split="lhs")
    dt_e = expanded[0:chunk]
    ecs_e = expanded[chunk:2 * chunk]
    dec_e = expanded[2 * chunk:3 * chunk]
    xdt = xs * dt_e

    def pad_time(v):
        if sp == chunk:
            return v
        return jnp.concatenate([v, jnp.zeros((sp - chunk, v.shape[1]), v.dtype)], axis=0)

    eye = (lax.broadcasted_iota(jnp.int32, (LANES, LANES), 0) == lax.broadcasted_iota(jnp.int32, (LANES, LANES), 1))
    cs_t = _exact_dot(jnp.where(eye, 1.0, 0.0).astype(BF16), pad_time(cs), dims=_NT)
    xdt_p = pad_time(xdt.astype(BF16))
    bm_p = pad_time(bm)
    causal = lax.broadcasted_iota(jnp.int32, (chunk, sp), 1) <= lax.broadcasted_iota(jnp.int32, (chunk, sp), 0)
    ecl = jnp.exp(cs_last)

    y_groups = []
    for g in range(groups):
        cg = cm[:, g * d_state:(g + 1) * d_state]
        cb_mat = lax.dot_general(cg, bm_p[:, g * d_state:(g + 1) * d_state], _NT, preferred_element_type=F32)
        sg = state_ref[sq, g * gw:(g + 1) * gw, :]
        y_off = lax.dot_general(cg, sg.astype(BF16), _NT, preferred_element_type=F32)
        y_heads = []
        for hh in range(hpg):
            hd = g * hpg + hh
            seg = jnp.exp(jnp.where(causal, cs[:, hd:hd + 1] - cs_t[hd:hd + 1, :], -jnp.inf))
            y_heads.append(_dot((cb_mat * seg).astype(BF16), xdt_p[:, hd * headdim:(hd + 1) * headdim]))
        y_groups.append(jnp.concatenate(y_heads, axis=1) + y_off * ecs_e[:, g * gw:(g + 1) * gw])
        xd = (xdt[:, g * gw:(g + 1) * gw] * dec_e[:, g * gw:(g + 1) * gw]).astype(BF16)
        upd = lax.dot_general(xd, bm[:, g * d_state:(g + 1) * d_state], _TN, preferred_element_type=F32)
        for hh in range(hpg):
            hd = g * hpg + hh
            r0 = g * gw + hh * headdim
            state_ref[sq, r0:r0 + headdim, :] = (sg[hh * headdim:(hh + 1) * headdim, :] * ecl[0:1, hd:hd + 1]
                                                 + upd[hh * headdim:(hh + 1) * headdim, :])

    yv = jnp.concatenate(y_groups, axis=1) + xs * dsk_ref[...]
    zv = z_ref[sq]
    yv = yv * (zv * jax.nn.sigmoid(zv))
    gv = g_ref[...]
    outs = [_rms(yv[:, g * gw:(g + 1) * gw], gv[:, g * gw:(g + 1) * gw]) for g in range(groups)]
    y_ref[sq] = jnp.concatenate(outs, axis=1).astype(BF16)

    @pl.when(c == last)
    def _():
        snew_ref[sq] = state_ref[sq]


def _ssd(xbc, z, dt, conv_prev, state_prev, w, *, chunk, l_real):
    bsz, t, cd = xbc.shape
    heads, headdim, groups, d_state = w["ssm_heads"], w["ssm_headdim"], w["ssm_groups"], w["ssm_state"]
    d_inner = heads * headdim
    nc = t // chunk
    nseq = math.gcd(SSD_SEQS, bsz)
    tok = lambda b, c: (b, c, 0)
    seq = lambda b, c: (b, 0, 0)
    const = lambda b, c: (0, 0)
    consts = [w["conv_w"], w["conv_b"], w["dt_bias"], w["a_log"], w["d_skip"], w["g_ssm"], w["head_expand"]]
    body = functools.partial(_ssd_body, nseq=nseq, chunk=chunk, l_real=l_real, heads=heads, headdim=headdim,
                             groups=groups, d_state=d_state)
    return pl.pallas_call(
        body,
        out_shape=[jax.ShapeDtypeStruct((bsz, t, d_inner), BF16),
                   jax.ShapeDtypeStruct((bsz, CONV_PAD, cd), F32),
                   jax.ShapeDtypeStruct((bsz, d_inner, d_state), F32)],
        grid=(bsz // nseq, nc),
        in_specs=[
            pl.BlockSpec((nseq, chunk, cd), tok),
            pl.BlockSpec((nseq, chunk, d_inner), tok),
            pl.BlockSpec((nseq, chunk, LANES), tok),
            pl.BlockSpec((nseq, CONV_PAD, cd), seq),
            pl.BlockSpec((nseq, d_inner, d_state), seq),
        ] + [pl.BlockSpec(v.shape, const) for v in consts],
        out_specs=[pl.BlockSpec((nseq, chunk, d_inner), tok),
                   pl.BlockSpec((nseq, CONV_PAD, cd), seq),
                   pl.BlockSpec((nseq, d_inner, d_state), seq)],
        scratch_shapes=[pltpu.VMEM((nseq, d_inner, d_state), F32), pltpu.VMEM((nseq, CONV_PAD + chunk, cd), F32)],
        compiler_params=_params("parallel", "arbitrary"),
        name="ssd_mixer",
    )(xbc, z, dt, conv_prev, state_prev, *consts)


def _out_proj_body(x_ref, oa_ref, os_ref, wa_ref, ws_ref, o_ref):
    o_ref[...] = x_ref[...] + _dot(oa_ref[...], wa_ref[...]) + _dot(os_ref[...], ws_ref[...])


def _out_proj(x, oa, osm, wa, ws):
    n, d = x.shape
    tm = min(ROW_TILE, n)
    row = lambda i: (i, 0)
    const = lambda i: (0, 0)
    return pl.pallas_call(
        _out_proj_body,
        out_shape=jax.ShapeDtypeStruct((n, d), F32),
        grid=(n // tm,),
        in_specs=[pl.BlockSpec((tm, d), row), pl.BlockSpec((tm, oa.shape[1]), row), pl.BlockSpec((tm, osm.shape[1]), row),
                  pl.BlockSpec(wa.shape, const), pl.BlockSpec(ws.shape, const)],
        out_specs=pl.BlockSpec((tm, d), row),
        compiler_params=_params("parallel"),
        name="out_proj",
    )(x, oa, osm, wa, ws)


def _ple_body(x_ref, pe_ref, g_ref, wg_ref, wp_ref, gf_ref, o_ref, *, final):
    x = x_ref[...]
    gate = jax.nn.sigmoid(_dot(_rms(x, g_ref[...]).astype(BF16), wg_ref[...]))
    x = x + gate * _dot(pe_ref[...].astype(BF16), wp_ref[...])
    o_ref[...] = _rms(x, gf_ref[...]) if final else x


def _ple(x, pe, g, wg, wp, g_final, final):
    n, d = x.shape
    tm = min(ROW_TILE, n)
    row = lambda i: (i, 0)
    const = lambda i: (0, 0)
    return pl.pallas_call(
        functools.partial(_ple_body, final=final),
        out_shape=jax.ShapeDtypeStruct((n, d), F32),
        grid=(n // tm,),
        in_specs=[pl.BlockSpec((tm, d), row), pl.BlockSpec((tm, pe.shape[1]), row), pl.BlockSpec((1, d), const),
                  pl.BlockSpec(wg.shape, const), pl.BlockSpec(wp.shape, const), pl.BlockSpec((1, d), const)],
        out_specs=pl.BlockSpec((tm, d), row),
        compiler_params=_params("parallel"),
        name="ple_final",
    )(x, pe, g, wg, wp, g_final)


def _rope_tables(pos, rope, heads):
    half = rope // 2
    inv = ROPE_THETA ** (-jnp.arange(half, dtype=F32) / half)
    ang = pos.astype(F32)[:, None] * inv[None, :]
    cos, sin = jnp.cos(ang), jnp.sin(ang)
    tile = lambda t: jnp.tile(jnp.concatenate([t, t], axis=-1), (1, heads))
    return tile(cos), tile(sin)


def _rot_cols(w, rope):
    d = w.shape[0]
    wr = w.reshape(d, -1, 2, rope // 2)
    return jnp.stack([-wr[:, :, 1], wr[:, :, 0]], axis=2).reshape(d, -1)


def _prep_layer(i, p):
    d = p["w_in"].shape[1]
    lat, heads, nope = p["w_uk"].shape[1:]
    vd = p["w_uv"].shape[3]
    rope = p["cache_krope"].shape[3]
    ssm_heads, headdim, d_state = p["state_ssm"].shape[2:]
    d_inner = ssm_heads * headdim
    conv_dim = p["conv_w"].shape[2]
    groups = (conv_dim - d_inner) // (2 * d_state)
    q_dim = heads * (nope + rope)
    w_in = p["w_in"][i]
    o = 0
    wq = w_in[:, o:o + q_dim].reshape(d, heads, nope + rope); o += q_dim
    w_ckv = w_in[:, o:o + lat]; o += lat
    w_kr = w_in[:, o:o + rope]; o += rope
    w_z = w_in[:, o:o + d_inner]; o += d_inner
    w_xbc = w_in[:, o:o + conv_dim]; o += conv_dim
    w_dt = w_in[:, o:o + ssm_heads]
    w_qr = wq[:, :, nope:].reshape(d, heads * rope)
    b = lambda t: t.astype(BF16)
    pad_heads = lambda v: jnp.pad(v.reshape(1, -1), ((0, 0), (0, LANES - ssm_heads)))
    head_of_channel = jnp.arange(d_inner) // headdim
    w_out = p["w_out"][i]
    return dict(
        heads=heads, nope=nope, rope=rope, lat=lat,
        ssm_heads=ssm_heads, ssm_headdim=headdim, ssm_groups=groups, ssm_state=d_state,
        g_ffn1=p["g_ffn1"][i][None], w1_a=p["w1_a"][i], w3_a=p["w3_a"][i], w2_a=p["w2_a"][i],
        g_ffn2=p["g_ffn2"][i][None], w1_b=p["w1_b"][i], w3_b=p["w3_b"][i], w2_b=p["w2_b"][i],
        g_mix=p["g_mix"][i][None],
        w_qn=b(wq[:, :, :nope].reshape(d, heads * nope)), w_qr=b(w_qr), w_qr_rot=b(_rot_cols(w_qr, rope)),
        w_ckv=b(w_ckv), w_kr=b(w_kr), w_kr_rot=b(_rot_cols(w_kr, rope)), g_kv=p["g_kv"][i][None],
        w_uk=b(p["w_uk"][i].reshape(lat, heads * nope)), w_uv=b(p["w_uv"][i].reshape(lat, heads * vd)),
        w_uk_t=b(jnp.transpose(p["w_uk"][i], (1, 2, 0))), w_uv_h=b(jnp.transpose(p["w_uv"][i], (1, 0, 2))),
        w_z=b(w_z), w_xbc=b(w_xbc), w_dt=b(jnp.pad(w_dt, ((0, 0), (0, LANES - ssm_heads)))),
        conv_w=p["conv_w"][i], conv_b=p["conv_b"][i][None], dt_bias=pad_heads(p["dt_bias"][i]),
        a_log=pad_heads(p["a_log"][i]), d_skip=jnp.repeat(p["d_skip"][i], headdim)[None], g_ssm=p["g_ssm"][i][None],
        head_expand=(jnp.arange(LANES)[:, None] == head_of_channel[None, :]).astype(BF16),
        w_out_a=b(w_out[:heads * vd]), w_out_s=b(w_out[heads * vd:]),
        g_ple=p["g_ple"][i][None], w_ple_gate=b(p["w_ple_gate"][i]), w_ple_proj=b(p["w_ple_proj"][i]),
    )


def _trunk_layer(x, pe, cos, sin, conv_prev, state_prev, attn, w, g_final, *, bsz, chunk, l_real, final):
    n, d = x.shape
    t = n // bsz
    scale = (w["nope"] + w["rope"]) ** -0.5
    x = _ffn(x, w["g_ffn1"], w["w1_a"], w["w3_a"], w["w2_a"])
    o_att, ckv, kr = attn(x, cos, sin, scale)
    z, xbc, dt = _mix_ssm(x, w["g_mix"], w)
    seq = lambda v: v.reshape(bsz, t, v.shape[1])
    if t % chunk:
        pad = lambda v: jnp.pad(seq(v), ((0, 0), (0, chunk - t), (0, 0)))
    else:
        pad = seq
    o_ssm, conv_new, ssm_new = _ssd(pad(xbc), pad(z), pad(dt), conv_prev, state_prev, w, chunk=chunk, l_real=l_real)
    o_ssm = o_ssm[:, :t].reshape(n, -1)
    x = _out_proj(x, o_att, o_ssm, w["w_out_a"], w["w_out_s"])
    x = _ffn(x, w["g_ffn2"], w["w1_b"], w["w3_b"], w["w2_b"])
    x = _ple(x, pe, w["g_ple"], w["w_ple_gate"], w["w_ple_proj"], g_final, final)
    return x, ckv, kr, conv_new[:, CONV_PAD - (CONV_WIDTH - 1):], ssm_new


def kernel(x_prompt, x_sample, p_prompt, p_sample, cache_ckv, cache_krope, state_conv, state_ssm, page_table, g_ffn1, w1_a, w3_a, w2_a, g_mix, w_in, g_kv, w_uk, w_uv, conv_w, conv_b, dt_bias, a_log, d_skip, g_ssm, w_out, g_ffn2, w1_b, w3_b, w2_b, g_ple, w_ple_gate, w_ple_proj, g_final):
    params = dict(cache_krope=cache_krope, state_ssm=state_ssm, g_ffn1=g_ffn1, w1_a=w1_a, w3_a=w3_a, w2_a=w2_a,
                  g_mix=g_mix, w_in=w_in, g_kv=g_kv, w_uk=w_uk, w_uv=w_uv, conv_w=conv_w, conv_b=conv_b,
                  dt_bias=dt_bias, a_log=a_log, d_skip=d_skip, g_ssm=g_ssm, w_out=w_out, g_ffn2=g_ffn2, w1_b=w1_b,
                  w3_b=w3_b, w2_b=w2_b, g_ple=g_ple, w_ple_gate=w_ple_gate, w_ple_proj=w_ple_proj)
    depth = w_in.shape[0]
    b_p, s_p, d = x_prompt.shape
    b_s, t_s, _ = x_sample.shape
    heads, rope = w_uk.shape[2], cache_krope.shape[3]
    lat = w_uk.shape[1]
    ssm_heads, headdim, d_state = state_ssm.shape[2:]
    d_inner = ssm_heads * headdim
    conv_dim = conv_w.shape[2]
    past_len = page_table.shape[1] * cache_ckv.shape[2]
    cos_p, sin_p = _rope_tables(jnp.arange(s_p), rope, heads)
    cos_s, sin_s = _rope_tables(past_len + jnp.arange(t_s), rope, heads)
    cos_s, sin_s = jnp.tile(cos_s, (b_s, 1)), jnp.tile(sin_s, (b_s, 1))
    chunk_p = math.gcd(s_p, SSD_CHUNK)
    chunk_s = -(-t_s // BF16_ROWS) * BF16_ROWS
    g_fin = g_final[None]
    cache_krope_t = jnp.swapaxes(cache_krope, 2, 3)
    hp = x_prompt.reshape(b_p * s_p, d)
    hs = x_sample.reshape(b_s * t_s, d)
    outs = [[] for _ in range(8)]
    for i in range(depth):
        w = _prep_layer(i, params)
        final = i == depth - 1

        def attn_p(x, cos, sin, scale):
            q3, k3, v3, ckv, kr = _mix_attn(x, w["g_mix"], cos, sin, w, prompt=True, scale=scale * LOG2_E)
            return _attn_prompt(q3, k3, v3, b_p), ckv, kr

        def attn_s(x, cos, sin, scale):
            ql, qr, ckv, kr = _mix_attn(x, w["g_mix"], cos, sin, w, prompt=False, scale=scale)
            new = lambda v: jnp.pad(v.reshape(b_s, t_s, -1), ((0, 0), (0, chunk_s - t_s), (0, 0))).astype(BF16)
            ol = _attn_decode(i, page_table, ql.reshape(b_s, t_s * heads, lat), qr.reshape(b_s, t_s * heads, rope),
                              new(ckv), new(kr), cache_ckv, cache_krope_t, heads)
            return _uv_proj(ol.reshape(b_s * t_s, heads * lat), w["w_uv_h"]), ckv, kr

        conv0 = jnp.zeros((b_p, CONV_PAD, conv_dim), F32)
        ssm0 = jnp.zeros((b_p, d_inner, d_state), F32)
        hp, c1, k1, cv1, s1 = _trunk_layer(hp, p_prompt[i].reshape(b_p * s_p, -1), cos_p, sin_p, conv0, ssm0, attn_p, w,
                                           g_fin, bsz=b_p, chunk=chunk_p, l_real=chunk_p, final=final)
        conv_s = jnp.pad(state_conv[i], ((0, 0), (CONV_PAD - (CONV_WIDTH - 1), 0), (0, 0)))
        hs, c2, k2, cv2, s2 = _trunk_layer(hs, p_sample[i].reshape(b_s * t_s, -1), cos_s, sin_s, conv_s,
                                           state_ssm[i].reshape(b_s, d_inner, d_state), attn_s, w, g_fin,
                                           bsz=b_s, chunk=chunk_s, l_real=t_s, final=final)
        for lst, v in zip(outs, (c1.reshape(b_p, s_p, -1), k1.reshape(b_p, s_p, -1), cv1,
                                 s1.reshape(b_p, ssm_heads, headdim, d_state),
                                 c2.reshape(b_s, t_s, -1), k2.reshape(b_s, t_s, -1), cv2,
                                 s2.reshape(b_s, ssm_heads, headdim, d_state))):
            lst.append(v)
    return (hp.reshape(b_p, s_p, d), hs.reshape(b_s, t_s, d)) + tuple(jnp.stack(o) for o in outs)
```

```python
import functools
import math

import jax
import jax.numpy as jnp
from jax import lax
from jax.experimental import pallas as pl
from jax.experimental.pallas import tpu as pltpu

F32 = jnp.float32
BF16 = jnp.bfloat16

EPS = 1e-6
HALF_STEP = 0.5
ROPE_THETA = 10000.0
SSD_CHUNK = 128
CONV_WIDTH = 4
LANES = 128
BF16_ROWS = 16
CONV_PAD = 8
V7X_VMEM_LIMIT = 56 * 1024 * 1024

FFN_ROWS = 1024
FFN_COLS = 256
MIX_ROWS = 256
ROW_TILE = 512
ATTN_Q = 1024
ATTN_K = 1024
ATTN_HEADS = 2
LOG2_E = math.log2(math.e)
DECODE_PAGES = 32
DECODE_GROUP = 8
SSD_SEQS = 4

_NT = (((1,), (1,)), ((), ()))
_TN = (((0,), (0,)), ((), ()))


def _params(*sem):
    return pltpu.CompilerParams(dimension_semantics=sem, vmem_limit_bytes=V7X_VMEM_LIMIT)


def _rms(x, g):
    return x * lax.rsqrt(jnp.mean(x * x, axis=-1, keepdims=True) + EPS) * g


def _dot(a, b):
    return jnp.dot(a, b, preferred_element_type=F32)


def _dot_t(a, b):
    return lax.dot_general(a, b, _NT, preferred_element_type=F32)


def _split3(v):
    hi = v.astype(BF16)
    r1 = v - hi.astype(F32)
    mid = r1.astype(BF16)
    lo = (r1 - mid.astype(F32)).astype(BF16)
    return hi, mid, lo


def _exact_dot(a, b, dims=None, split="rhs"):
    f = (lambda x, y: _dot(x, y)) if dims is None else (lambda x, y: lax.dot_general(x, y, dims, preferred_element_type=F32))
    if split == "rhs":
        h, m, l = _split3(b)
        return f(a, h) + f(a, m) + f(a, l)
    h, m, l = _split3(a)
    return f(h, b) + f(m, b) + f(l, b)


def _ffn_body(x_ref, g_ref, w1_ref, w3_ref, w2_ref, o_ref, xn_ref):
    @pl.when(pl.program_id(1) == 0)
    def _():
        x = x_ref[...]
        xn_ref[...] = _rms(x, g_ref[...]).astype(BF16)
        o_ref[...] = x

    xn = xn_ref[...]
    h1 = _dot(xn, w1_ref[...].astype(BF16))
    h3 = _dot(xn, w3_ref[...].astype(BF16))
    act = (h1 * jax.nn.sigmoid(h1) * h3).astype(BF16)
    o_ref[...] += HALF_STEP * _dot(act, w2_ref[...].astype(BF16))


def _ffn(x, g, w1, w3, w2):
    n, d = x.shape
    f = w1.shape[1]
    tm, tf = min(FFN_ROWS, n), FFN_COLS
    return pl.pallas_call(
        _ffn_body,
        out_shape=jax.ShapeDtypeStruct((n, d), F32),
        grid=(n // tm, f // tf),
        in_specs=[
            pl.BlockSpec((tm, d), lambda i, j: (i, 0)),
            pl.BlockSpec((1, d), lambda i, j: (0, 0)),
            pl.BlockSpec((d, tf), lambda i, j: (0, j)),
            pl.BlockSpec((d, tf), lambda i, j: (0, j)),
            pl.BlockSpec((tf, d), lambda i, j: (j, 0)),
        ],
        out_specs=pl.BlockSpec((tm, d), lambda i, j: (i, 0)),
        scratch_shapes=[pltpu.VMEM((tm, d), BF16)],
        compiler_params=_params("parallel", "arbitrary"),
        name="ffn_half_step",
    )(x, g, w1, w3, w2)


def _latent_and_rope_key(h, cos, sin, wckv_ref, wkr_ref, wkrr_ref, gkv_ref):
    ckv = _rms(_dot_t(h, wckv_ref[...]), gkv_ref[...])
    kr = _dot_t(h, wkr_ref[...]) * cos + _dot_t(h, wkrr_ref[...]) * sin
    return ckv, kr


def _mix_attn_prompt_body(x_ref, g_ref, cos_ref, sin_ref, cost_ref, sint_ref, wqnt_ref, wqrt_ref, wqrrt_ref,
                          wckv_ref, wkr_ref, wkrr_ref, gkv_ref, wuk_ref, wuvt_ref,
                          qt_ref, k3_ref, vt_ref, ckv_ref, kr_ref, *, heads, nope, rope, scale):
    h = _rms(x_ref[...], g_ref[...]).astype(BF16)
    qn_t = _dot_t(wqnt_ref[...], h) * scale
    qr_t = (_dot_t(wqrt_ref[...], h) * cost_ref[...] + _dot_t(wqrrt_ref[...], h) * sint_ref[...]) * scale
    ckv, kr = _latent_and_rope_key(h, cos_ref[...], sin_ref[...], wckv_ref, wkr_ref, wkrr_ref, gkv_ref)
    ckv_b = ckv.astype(BF16)
    kn = _dot(ckv_b, wuk_ref[...])
    v_t = _dot_t(wuvt_ref[...], ckv_b)
    kr_b = kr.astype(BF16)
    vd = v_t.shape[0] // heads
    for hd in range(heads):
        qt_ref[hd, 0:nope, :] = qn_t[hd * nope:(hd + 1) * nope, :].astype(BF16)
        qt_ref[hd, nope:nope + rope, :] = qr_t[hd * rope:(hd + 1) * rope, :].astype(BF16)
        k3_ref[hd, :, 0:nope] = kn[:, hd * nope:(hd + 1) * nope].astype(BF16)
        k3_ref[hd, :, nope:nope + rope] = kr_b
        vt_ref[hd] = v_t[hd * vd:(hd + 1) * vd, :].astype(BF16)
    ckv_ref[...] = ckv
    kr_ref[...] = kr


def _mix_attn_sample_body(x_ref, g_ref, cos_ref, sin_ref, wqn_ref, wqr_ref, wqrr_ref, wckv_ref, wkr_ref, wkrr_ref,
                          gkv_ref, wukt_ref, ql_ref, qr_ref, ckv_ref, kr_ref, *, heads, nope, rope, scale):
    h = _rms(x_ref[...], g_ref[...]).astype(BF16)
    cos = cos_ref[...]
    sin = sin_ref[...]
    qn = _dot_t(h, wqn_ref[...]) * scale
    qr = (_dot_t(h, wqr_ref[...]) * cos + _dot_t(h, wqrr_ref[...]) * sin) * scale
    ckv, kr = _latent_and_rope_key(h, cos[:, :rope], sin[:, :rope], wckv_ref, wkr_ref, wkrr_ref, gkv_ref)
    lat = wukt_ref.shape[2]
    for hd in range(heads):
        qh = qn[:, hd * nope:(hd + 1) * nope].astype(BF16)
        ql_ref[:, hd * lat:(hd + 1) * lat] = _dot(qh, wukt_ref[hd]).astype(BF16)
    qr_ref[...] = qr.astype(BF16)
    ckv_ref[...] = ckv
    kr_ref[...] = kr


def _mix_attn_prompt(x, g, cos, sin, w, scale):
    n, d = x.shape
    heads, nope, rope, lat = w["heads"], w["nope"], w["rope"], w["lat"]
    tm = min(MIX_ROWS, n)
    period = cos.shape[0] // tm
    cos_t, sin_t = jnp.tile(cos, (1, heads)).T, jnp.tile(sin, (1, heads)).T
    vd = w["w_uv_t"].shape[0] // heads
    const = lambda i: (0, 0)
    row = lambda i: (i, 0)
    ws = [w["w_qn_t"], w["w_qr_t"], w["w_qr_rot_t"], w["w_ckv_t"], w["w_kr_t"], w["w_kr_rot_t"], w["g_kv"],
          w["w_uk"], w["w_uv_t"]]
    body = functools.partial(_mix_attn_prompt_body, heads=heads, nope=nope, rope=rope, scale=scale)
    return pl.pallas_call(
        body,
        out_shape=[jax.ShapeDtypeStruct((heads, nope + rope, n), BF16),
                   jax.ShapeDtypeStruct((heads, n, nope + rope), BF16),
                   jax.ShapeDtypeStruct((heads, vd, n), BF16),
                   jax.ShapeDtypeStruct((n, lat), F32),
                   jax.ShapeDtypeStruct((n, rope), F32)],
        grid=(n // tm,),
        in_specs=[
            pl.BlockSpec((tm, d), row),
            pl.BlockSpec((1, d), const),
            pl.BlockSpec((tm, rope), lambda i: (i % period, 0)),
            pl.BlockSpec((tm, rope), lambda i: (i % period, 0)),
            pl.BlockSpec((heads * rope, tm), lambda i: (0, i % period)),
            pl.BlockSpec((heads * rope, tm), lambda i: (0, i % period)),
        ] + [pl.BlockSpec(v.shape, const) for v in ws],
        out_specs=[pl.BlockSpec((heads, nope + rope, tm), lambda i: (0, 0, i)),
                   pl.BlockSpec((heads, tm, nope + rope), lambda i: (0, i, 0)),
                   pl.BlockSpec((heads, vd, tm), lambda i: (0, 0, i)),
                   pl.BlockSpec((tm, lat), row),
                   pl.BlockSpec((tm, rope), row)],
        compiler_params=_params("parallel"),
        name="mix_attn_prompt",
    )(x, g, cos, sin, cos_t, sin_t, *ws)


def _mix_attn_sample(x, g, cos, sin, w, scale):
    n, d = x.shape
    heads, nope, rope, lat = w["heads"], w["nope"], w["rope"], w["lat"]
    tm = min(MIX_ROWS, n)
    const = lambda i: (0, 0)
    row = lambda i: (i, 0)
    ws = [w["w_qn_t"], w["w_qr_t"], w["w_qr_rot_t"], w["w_ckv_t"], w["w_kr_t"], w["w_kr_rot_t"], w["g_kv"]]
    body = functools.partial(_mix_attn_sample_body, heads=heads, nope=nope, rope=rope, scale=scale)
    return pl.pallas_call(
        body,
        out_shape=[jax.ShapeDtypeStruct((n, heads * lat), BF16),
                   jax.ShapeDtypeStruct((n, heads * rope), BF16),
                   jax.ShapeDtypeStruct((n, lat), F32),
                   jax.ShapeDtypeStruct((n, rope), F32)],
        grid=(n // tm,),
        in_specs=[pl.BlockSpec((tm, d), row), pl.BlockSpec((1, d), const),
                  pl.BlockSpec((tm, heads * rope), row), pl.BlockSpec((tm, heads * rope), row)]
                 + [pl.BlockSpec(v.shape, const) for v in ws]
                 + [pl.BlockSpec(w["w_uk_t"].shape, lambda i: (0, 0, 0))],
        out_specs=[pl.BlockSpec((tm, heads * lat), row), pl.BlockSpec((tm, heads * rope), row),
                   pl.BlockSpec((tm, lat), row), pl.BlockSpec((tm, rope), row)],
        compiler_params=_params("parallel"),
        name="mix_attn_sample",
    )(x, g, cos, sin, *ws, w["w_uk_t"])


def _mix_ssm_body(x_ref, g_ref, wz_ref, wx_ref, wdt_ref, z_ref, xbc_ref, dt_ref):
    h = _rms(x_ref[...], g_ref[...]).astype(BF16)
    z_ref[...] = _dot_t(h, wz_ref[...])
    xbc_ref[...] = _dot_t(h, wx_ref[...])
    dt_ref[...] = _dot_t(h, wdt_ref[...])


def _mix_ssm(x, g, w):
    n, d = x.shape
    tm = min(MIX_ROWS, n)
    const = lambda i: (0, 0)
    row = lambda i: (i, 0)
    ws = [w["w_z_t"], w["w_xbc_t"], w["w_dt_t"]]
    return pl.pallas_call(
        _mix_ssm_body,
        out_shape=[jax.ShapeDtypeStruct((n, wi.shape[0]), F32) for wi in ws],
        grid=(n // tm,),
        in_specs=[pl.BlockSpec((tm, d), row), pl.BlockSpec((1, d), const)] + [pl.BlockSpec(wi.shape, const) for wi in ws],
        out_specs=[pl.BlockSpec((tm, wi.shape[0]), row) for wi in ws],
        compiler_params=_params("parallel"),
        name="mix_ssm",
    )(x, g, *ws)


def _softmax_step_t(q_t, kb, vb_t, carry, mask):
    m, l, acc_t = carry
    s_t = _dot(kb, q_t)
    if mask is not None:
        s_t = jnp.where(mask, s_t, -jnp.inf)
    m_new = jnp.maximum(m, jnp.max(s_t, axis=0, keepdims=True))
    alpha = jnp.exp2(m - m_new)
    p_t = jnp.exp2(s_t - m_new)
    l = alpha * l + jnp.sum(p_t, axis=0, keepdims=True)
    acc_t = alpha * acc_t + _dot(vb_t, p_t.astype(BF16))
    return m_new, l, acc_t


def _attn_body(qt_ref, k_ref, vt_ref, o_ref, *, tq, tk, nh):
    i = pl.program_id(2)
    vd = vt_ref.shape[1]
    qts = [qt_ref[h] for h in range(nh)]

    def blocks(off, carries, mask):
        return tuple(_softmax_step_t(qts[h], k_ref[h, pl.ds(off, tk), :], vt_ref[h, :, pl.ds(off, tk)], carries[h], mask)
                     for h in range(nh))

    def full_block(j, carries):
        return blocks(pl.multiple_of(j * tk, tk), carries, None)

    carries = tuple((jnp.full((1, tq), -jnp.inf, F32), jnp.zeros((1, tq), F32), jnp.zeros((vd, tq), F32))
                    for _ in range(nh))
    carries = lax.fori_loop(0, i * (tq // tk), full_block, carries)
    keys = lax.broadcasted_iota(jnp.int32, (tk, tq), 0)
    queries = lax.broadcasted_iota(jnp.int32, (tk, tq), 1)
    for dblk in range(tq // tk):
        carries = blocks(pl.multiple_of(i * tq + dblk * tk, tk), carries, keys + dblk * tk <= queries)
    for h, (_, l, acc_t) in enumerate(carries):
        o_ref[:, h * vd:(h + 1) * vd] = (acc_t / l).T.astype(BF16)


def _attn_prompt(q3t, k3, v3t, batch):
    heads, dq, n = q3t.shape
    vd = v3t.shape[1]
    s = n // batch
    tq = min(ATTN_Q, s)
    tk = min(ATTN_K, tq)
    nq = s // tq
    nh = math.gcd(ATTN_HEADS, heads)
    return pl.pallas_call(
        functools.partial(_attn_body, tq=tq, tk=tk, nh=nh),
        out_shape=jax.ShapeDtypeStruct((n, heads * vd), BF16),
        grid=(batch, heads // nh, nq),
        in_specs=[
            pl.BlockSpec((nh, dq, tq), lambda b, h, i: (h, 0, b * nq + i)),
            pl.BlockSpec((nh, s, dq), lambda b, h, i: (h, b, 0)),
            pl.BlockSpec((nh, vd, s), lambda b, h, i: (h, 0, b)),
        ],
        out_specs=pl.BlockSpec((tq, nh * vd), lambda b, h, i: (b * nq + i, h)),
        compiler_params=_params("parallel", "parallel", "arbitrary"),
        name="attn_prompt",
    )(q3t, k3, v3t)


def _decode_body(pt_ref, ql_ref, qr_ref, cn_ref, kn_ref, ckv_hbm, krt_hbm, o_ref,
                 ckv_buf, krt_buf, sem, m_ref, l_ref, acc_ref, *, layer, pages, group, heads):
    b, p = pl.program_id(0), pl.program_id(1)
    nb, ns = pl.num_programs(0), pl.num_programs(1)
    t = b * ns + p
    slot = lax.rem(t, 2)
    page = ckv_buf.shape[1] // pages

    def page_copies(bb, pp, sl, k):
        pid = pt_ref[bb, pp * pages + k]
        rows = pl.ds(k * page, page)
        return (pltpu.make_async_copy(ckv_hbm.at[layer, pid], ckv_buf.at[sl, rows, :], sem.at[sl, 0]),
                pltpu.make_async_copy(krt_hbm.at[layer, pid], krt_buf.at[sl, :, rows], sem.at[sl, 1]))

    @pl.when(t == 0)
    def _():
        for k in range(pages):
            for c in page_copies(0, 0, 0, k):
                c.start()

    for k in range(pages):
        for c in page_copies(b, p, slot, k):
            c.wait()

    @pl.when(p == 0)
    def _():
        m_ref[...] = jnp.full_like(m_ref, -jnp.inf)
        l_ref[...] = jnp.zeros_like(l_ref)
        acc_ref[...] = jnp.zeros_like(acc_ref)

    last_p = p == ns - 1
    bn = jnp.where(last_p, jnp.where(b == nb - 1, 0, b + 1), b)
    pn = jnp.where(last_p, 0, p + 1)
    ql = ql_ref[0]
    qr = qr_ref[0]

    def update(s, kv):
        m = m_ref[...]
        m_new = jnp.maximum(m, jnp.max(s, axis=-1, keepdims=True))
        alpha = jnp.exp(m - m_new)
        pr = jnp.exp(s - m_new)
        l_ref[...] = alpha * l_ref[...] + jnp.sum(pr, axis=-1, keepdims=True)
        acc_ref[...] = alpha * acc_ref[...] + _dot(pr.astype(BF16), kv)
        m_ref[...] = m_new

    s_parts, kv_parts = [], []
    for j in range(pages // group):
        for k in range(j * group, (j + 1) * group):
            for c in page_copies(bn, pn, 1 - slot, k):
                c.start()
        rows = pl.ds(j * group * page, group * page)
        kv_j = ckv_buf[slot, rows, :].astype(BF16)
        s_parts.append(lax.dot_general(ql, kv_j, _NT, preferred_element_type=F32)
                       + _dot(qr, krt_buf[slot, :, rows].astype(BF16)))
        kv_parts.append(kv_j)
    update(jnp.concatenate(s_parts, axis=1), jnp.concatenate(kv_parts, axis=0))

    @pl.when(last_p)
    def _():
        rows, nk = ql.shape[0], cn_ref.shape[1]
        qtok = lax.broadcasted_iota(jnp.int32, (rows, nk), 0) // heads
        ktok = lax.broadcasted_iota(jnp.int32, (rows, nk), 1)
        kvn = cn_ref[0]
        s = (lax.dot_general(ql, kvn, _NT, preferred_element_type=F32)
             + lax.dot_general(qr, kn_ref[0], _NT, preferred_element_type=F32))
        update(jnp.where(ktok <= qtok, s, -jnp.inf), kvn)
        o_ref[0] = (acc_ref[...] / l_ref[...]).astype(BF16)

    @pl.when(t == nb * ns - 1)
    def _():
        for k in range(pages):
            for c in page_copies(bn, pn, 1 - slot, k):
                c.wait()


def _attn_decode(layer, page_table, ql, qr, ckv_new, kr_new, cache_ckv, cache_krope_t, heads):
    bsz, rows, lat = ql.shape
    rope = qr.shape[2]
    n_pages = page_table.shape[1]
    page = cache_ckv.shape[2]
    g = math.gcd(DECODE_PAGES, n_pages)
    nk = ckv_new.shape[1]
    bmap = lambda b, p, pt: (b, 0, 0)
    grid_spec = pltpu.PrefetchScalarGridSpec(
        num_scalar_prefetch=1,
        grid=(bsz, n_pages // g),
        in_specs=[
            pl.BlockSpec((1, rows, lat), bmap),
            pl.BlockSpec((1, rows, rope), bmap),
            pl.BlockSpec((1, nk, lat), bmap),
            pl.BlockSpec((1, nk, rope), bmap),
            pl.BlockSpec(memory_space=pl.ANY),
            pl.BlockSpec(memory_space=pl.ANY),
        ],
        out_specs=pl.BlockSpec((1, rows, lat), bmap),
        scratch_shapes=[pltpu.VMEM((2, g * page, lat), F32), pltpu.VMEM((2, rope, g * page), F32),
                        pltpu.SemaphoreType.DMA((2, 2)),
                        pltpu.VMEM((rows, 1), F32), pltpu.VMEM((rows, 1), F32), pltpu.VMEM((rows, lat), F32)],
    )
    return pl.pallas_call(
        functools.partial(_decode_body, layer=layer, pages=g, group=math.gcd(DECODE_GROUP, g), heads=heads),
        grid_spec=grid_spec,
        out_shape=jax.ShapeDtypeStruct((bsz, rows, lat), BF16),
        compiler_params=_params("arbitrary", "arbitrary"),
        name="attn_decode",
    )(page_table, ql, qr, ckv_new, kr_new, cache_ckv, cache_krope_t)


def _uv_body(ol_ref, wuv_ref, o_ref, *, heads):
    lat, vd = wuv_ref.shape[1], wuv_ref.shape[2]
    for hd in range(heads):
        o_ref[:, hd * vd:(hd + 1) * vd] = _dot(ol_ref[:, hd * lat:(hd + 1) * lat], wuv_ref[hd]).astype(BF16)


def _uv_proj(ol, wuv_h):
    n = ol.shape[0]
    heads, lat, vd = wuv_h.shape
    tm = min(ROW_TILE, n)
    return pl.pallas_call(
        functools.partial(_uv_body, heads=heads),
        out_shape=jax.ShapeDtypeStruct((n, heads * vd), BF16),
        grid=(n // tm,),
        in_specs=[pl.BlockSpec((tm, heads * lat), lambda i: (i, 0)), pl.BlockSpec(wuv_h.shape, lambda i: (0, 0, 0))],
        out_specs=pl.BlockSpec((tm, heads * vd), lambda i: (i, 0)),
        compiler_params=_params("parallel"),
        name="uv_proj",
    )(ol, wuv_h)


def _ssd_body(xbc_ref, z_ref, dt_ref, cprev_ref, sprev_ref, cw_ref, cb_ref, dtb_ref, alog_ref, dsk_ref, g_ref, r_ref,
              y_ref, cnew_ref, snew_ref, state_ref, xp_ref, *, nseq, **dims):
    for sq in range(nseq):
        _ssd_seq(sq, xbc_ref, z_ref, dt_ref, cprev_ref, sprev_ref, cw_ref, cb_ref, dtb_ref, alog_ref, dsk_ref, g_ref,
                 r_ref, y_ref, cnew_ref, snew_ref, state_ref, xp_ref, **dims)


def _ssd_seq(sq, xbc_ref, z_ref, dt_ref, cprev_ref, sprev_ref, cw_ref, cb_ref, dtb_ref, alog_ref, dsk_ref, g_ref, r_ref,
             y_ref, cnew_ref, snew_ref, state_ref, xp_ref, *, chunk, l_real, heads, headdim, groups, d_state):
    c = pl.program_id(1)
    last = pl.num_programs(1) - 1
    d_inner = heads * headdim
    gw = d_inner // groups
    hpg = heads // groups
    sp = max(chunk, LANES)

    @pl.when(c == 0)
    def _():
        state_ref[sq] = sprev_ref[sq]
        xp_ref[sq, 0:CONV_PAD, :] = cprev_ref[sq]

    xp_ref[sq, CONV_PAD:CONV_PAD + chunk, :] = xbc_ref[sq]
    cw = cw_ref[...]
    y = cb_ref[...]
    for k in range(CONV_WIDTH):
        y = y + xp_ref[sq, CONV_PAD - (CONV_WIDTH - 1) + k:CONV_PAD - (CONV_WIDTH - 1) + k + chunk, :] * cw[k:k + 1, :]
    xbc = y * jax.nn.sigmoid(y)
    tail = xp_ref[sq, l_real:l_real + CONV_PAD, :]
    xp_ref[sq, 0:CONV_PAD, :] = tail

    @pl.when(c == last)
    def _():
        cnew_ref[sq] = tail

    xs = xbc[:, :d_inner]
    bm = xbc[:, d_inner:d_inner + groups * d_state].astype(BF16)
    cm = xbc[:, d_inner + groups * d_state:].astype(BF16)

    t_row = lax.broadcasted_iota(jnp.int32, (chunk, LANES), 0)
    dtr = dt_ref[sq] + dtb_ref[...]
    dt = jnp.maximum(dtr, 0.0) + jnp.log1p(jnp.exp(-jnp.abs(dtr)))
    if l_real < chunk:
        dt = jnp.where(t_row < l_real, dt, 0.0)
    a = dt * (-jnp.exp(alog_ref[...]))
    tri = (lax.broadcasted_iota(jnp.int32, (chunk, chunk), 0) >= lax.broadcasted_iota(jnp.int32, (chunk, chunk), 1))
    cs = _exact_dot(jnp.where(tri, 1.0, 0.0).astype(BF16), a)
    cs_last = cs[chunk - 1:chunk, :]
    ecs = jnp.exp(cs)
    dec = jnp.exp(cs_last - cs)
    expanded = _exact_dot(jnp.concatenate([dt, ecs, dec], axis=0), r_ref[...], split="lhs")
    dt_e = expanded[0:chunk]
    ecs_e = expanded[chunk:2 * chunk]
    dec_e = expanded[2 * chunk:3 * chunk]
    xdt = xs * dt_e

    def pad_time(v):
        if sp == chunk:
            return v
        return jnp.concatenate([v, jnp.zeros((sp - chunk, v.shape[1]), v.dtype)], axis=0)

    eye = (lax.broadcasted_iota(jnp.int32, (LANES, LANES), 0) == lax.broadcasted_iota(jnp.int32, (LANES, LANES), 1))
    cs_t = _exact_dot(jnp.where(eye, 1.0, 0.0).astype(BF16), pad_time(cs), dims=_NT)
    xdt_p = pad_time(xdt.astype(BF16))
    bm_p = pad_time(bm)
    causal = lax.broadcasted_iota(jnp.int32, (chunk, sp), 1) <= lax.broadcasted_iota(jnp.int32, (chunk, sp), 0)
    ecl = jnp.exp(cs_last)

    y_groups = []
    for g in range(groups):
        cg = cm[:, g * d_state:(g + 1) * d_state]
        cb_mat = lax.dot_general(cg, bm_p[:, g * d_state:(g + 1) * d_state], _NT, preferred_element_type=F32)
        sg = state_ref[sq, g * gw:(g + 1) * gw, :]
        y_off = lax.dot_general(cg, sg.astype(BF16), _NT, preferred_element_type=F32)
        y_heads = []
        for hh in range(hpg):
            hd = g * hpg + hh
            seg = jnp.exp(jnp.where(causal, cs[:, hd:hd + 1] - cs_t[hd:hd + 1, :], -jnp.inf))
            y_heads.append(_dot((cb_mat * seg).astype(BF16), xdt_p[:, hd * headdim:(hd + 1) * headdim]))
        y_groups.append(jnp.concatenate(y_heads, axis=1) + y_off * ecs_e[:, g * gw:(g + 1) * gw])
        xd = (xdt[:, g * gw:(g + 1) * gw] * dec_e[:, g * gw:(g + 1) * gw]).astype(BF16)
        upd = lax.dot_general(xd, bm[:, g * d_state:(g + 1) * d_state], _TN, preferred_element_type=F32)
        for hh in range(hpg):
            hd = g * hpg + hh
            r0 = g * gw + hh * headdim
            state_ref[sq, r0:r0 + headdim, :] = (sg[hh * headdim:(hh + 1) * headdim, :] * ecl[0:1, hd:hd + 1]
                                                 + upd[hh * headdim:(hh + 1) * headdim, :])

    yv = jnp.concatenate(y_groups, axis=1) + xs * dsk_ref[...]
    zv = z_ref[sq]
    yv = yv * (zv * jax.nn.sigmoid(zv))
    gv = g_ref[...]
    outs = [_rms(yv[:, g * gw:(g + 1) * gw], gv[:, g * gw:(g + 1) * gw]) for g in range(groups)]
    y_ref[sq] = jnp.concatenate(outs, axis=1).astype(BF16)

    @pl.when(c == last)
    def _():
        snew_ref[sq] = state_ref[sq]


def _ssd(xbc, z, dt, conv_prev, state_prev, w, *, chunk, l_real):
    bsz, t, cd = xbc.shape
    heads, headdim, groups, d_state = w["ssm_heads"], w["ssm_headdim"], w["ssm_groups"], w["ssm_state"]
    d_inner = heads * headdim
    nc = t // chunk
    nseq = math.gcd(SSD_SEQS, bsz)
    tok = lambda b, c: (b, c, 0)
    seq = lambda b, c: (b, 0, 0)
    const = lambda b, c: (0, 0)
    consts = [w["conv_w"], w["conv_b"], w["dt_bias"], w["a_log"], w["d_skip"], w["g_ssm"], w["head_expand"]]
    body = functools.partial(_ssd_body, nseq=nseq, chunk=chunk, l_real=l_real, heads=heads, headdim=headdim,
                             groups=groups, d_state=d_state)
    return pl.pallas_call(
        body,
        out_shape=[jax.ShapeDtypeStruct((bsz, t, d_inner), BF16),
                   jax.ShapeDtypeStruct((bsz, CONV_PAD, cd), F32),
                   jax.ShapeDtypeStruct((bsz, d_inner, d_state), F32)],
        grid=(bsz // nseq, nc),
        in_specs=[
            pl.BlockSpec((nseq, chunk, cd), tok),
            pl.BlockSpec((nseq, chunk, d_inner), tok),
            pl.BlockSpec((nseq, chunk, LANES), tok),
            pl.BlockSpec((nseq, CONV_PAD, cd), seq),
            pl.BlockSpec((nseq, d_inner, d_state), seq),
        ] + [pl.BlockSpec(v.shape, const) for v in consts],
        out_specs=[pl.BlockSpec((nseq, chunk, d_inner), tok),
                   pl.BlockSpec((nseq, CONV_PAD, cd), seq),
                   pl.BlockSpec((nseq, d_inner, d_state), seq)],
        scratch_shapes=[pltpu.VMEM((nseq, d_inner, d_state), F32), pltpu.VMEM((nseq, CONV_PAD + chunk, cd), F32)],
        compiler_params=_params("parallel", "arbitrary"),
        name="ssd_mixer",
    )(xbc, z, dt, conv_prev, state_prev, *consts)


def _out_proj_body(x_ref, oa_ref, os_ref, wa_ref, ws_ref, o_ref):
    o_ref[...] = x_ref[...] + _dot(oa_ref[...], wa_ref[...]) + _dot(os_ref[...], ws_ref[...])


def _out_proj(x, oa, osm, wa, ws):
    n, d = x.shape
    tm = min(ROW_TILE, n)
    row = lambda i: (i, 0)
    const = lambda i: (0, 0)
    return pl.pallas_call(
        _out_proj_body,
        out_shape=jax.ShapeDtypeStruct((n, d), F32),
        grid=(n // tm,),
        in_specs=[pl.BlockSpec((tm, d), row), pl.BlockSpec((tm, oa.shape[1]), row), pl.BlockSpec((tm, osm.shape[1]), row),
                  pl.BlockSpec(wa.shape, const), pl.BlockSpec(ws.shape, const)],
        out_specs=pl.BlockSpec((tm, d), row),
        compiler_params=_params("parallel"),
        name="out_proj",
    )(x, oa, osm, wa, ws)


def _ple_body(x_ref, pe_ref, g_ref, wg_ref, wp_ref, gf_ref, o_ref, *, final):
    x = x_ref[...]
    gate = jax.nn.sigmoid(_dot(_rms(x, g_ref[...]).astype(BF16), wg_ref[...]))
    x = x + gate * _dot(pe_ref[...].astype(BF16), wp_ref[...])
    o_ref[...] = _rms(x, gf_ref[...]) if final else x


def _ple(x, pe, g, wg, wp, g_final, final):
    n, d = x.shape
    tm = min(ROW_TILE, n)
    row = lambda i: (i, 0)
    const = lambda i: (0, 0)
    return pl.pallas_call(
        functools.partial(_ple_body, final=final),
        out_shape=jax.ShapeDtypeStruct((n, d), F32),
        grid=(n // tm,),
        in_specs=[pl.BlockSpec((tm, d), row), pl.BlockSpec((tm, pe.shape[1]), row), pl.BlockSpec((1, d), const),
                  pl.BlockSpec(wg.shape, const), pl.BlockSpec(wp.shape, const), pl.BlockSpec((1, d), const)],
        out_specs=pl.BlockSpec((tm, d), row),
        compiler_params=_params("parallel"),
        name="ple_final",
    )(x, pe, g, wg, wp, g_final)


def _rope_tables(pos, rope):
    half = rope // 2
    inv = ROPE_THETA ** (-jnp.arange(half, dtype=F32) / half)
    ang = pos.astype(F32)[:, None] * inv[None, :]
    cos, sin = jnp.cos(ang), jnp.sin(ang)
    return jnp.concatenate([cos, cos], axis=-1), jnp.concatenate([sin, sin], axis=-1)


def _rot_rows(w_t, rope):
    d = w_t.shape[1]
    wr = w_t.reshape(-1, 2, rope // 2, d)
    return jnp.stack([-wr[:, 1], wr[:, 0]], axis=1).reshape(-1, d)


def _prep_layer(i, p):
    d = p["w_in"].shape[1]
    lat, heads, nope = p["w_uk"].shape[1:]
    vd = p["w_uv"].shape[3]
    rope = p["cache_krope"].shape[3]
    ssm_heads, headdim, d_state = p["state_ssm"].shape[2:]
    d_inner = ssm_heads * headdim
    conv_dim = p["conv_w"].shape[2]
    groups = (conv_dim - d_inner) // (2 * d_state)
    q_dim = heads * (nope + rope)
    w_in_t = jnp.swapaxes(p["w_in"], 1, 2)[i]
    o = 0
    wq_t = w_in_t[o:o + q_dim].reshape(heads, nope + rope, d); o += q_dim
    w_ckv_t = w_in_t[o:o + lat]; o += lat
    w_kr_t = w_in_t[o:o + rope]; o += rope
    w_z_t = w_in_t[o:o + d_inner]; o += d_inner
    w_xbc_t = w_in_t[o:o + conv_dim]; o += conv_dim
    w_dt_t = w_in_t[o:o + ssm_heads]
    w_qn_t = wq_t[:, :nope].reshape(heads * nope, d)
    w_qr_t = wq_t[:, nope:].reshape(heads * rope, d)
    b = lambda t: t.astype(BF16)
    pad_heads = lambda v: jnp.pad(v.reshape(1, -1), ((0, 0), (0, LANES - ssm_heads)))
    head_of_channel = jnp.arange(d_inner) // headdim
    w_out = p["w_out"][i]
    return dict(
        heads=heads, nope=nope, rope=rope, lat=lat,
        ssm_heads=ssm_heads, ssm_headdim=headdim, ssm_groups=groups, ssm_state=d_state,
        g_ffn1=p["g_ffn1"][i][None], w1_a=p["w1_a"][i], w3_a=p["w3_a"][i], w2_a=p["w2_a"][i],
        g_ffn2=p["g_ffn2"][i][None], w1_b=p["w1_b"][i], w3_b=p["w3_b"][i], w2_b=p["w2_b"][i],
        g_mix=p["g_mix"][i][None],
        w_qn_t=b(w_qn_t), w_qr_t=b(w_qr_t), w_qr_rot_t=b(_rot_rows(w_qr_t, rope)),
        w_ckv_t=b(w_ckv_t), w_kr_t=b(w_kr_t), w_kr_rot_t=b(_rot_rows(w_kr_t, rope)), g_kv=p["g_kv"][i][None],
        w_uk=b(p["w_uk"][i].reshape(lat, heads * nope)), w_uv_t=b(p["w_uv"][i].reshape(lat, heads * vd).T),
        w_uk_t=b(jnp.transpose(p["w_uk"][i], (1, 2, 0))), w_uv_h=b(jnp.transpose(p["w_uv"][i], (1, 0, 2))),
        w_z_t=b(w_z_t), w_xbc_t=b(w_xbc_t), w_dt_t=b(jnp.pad(w_dt_t, ((0, LANES - ssm_heads), (0, 0)))),
        conv_w=p["conv_w"][i], conv_b=p["conv_b"][i][None], dt_bias=pad_heads(p["dt_bias"][i]),
        a_log=pad_heads(p["a_log"][i]), d_skip=jnp.repeat(p["d_skip"][i], headdim)[None], g_ssm=p["g_ssm"][i][None],
        head_expand=(jnp.arange(LANES)[:, None] == head_of_channel[None, :]).astype(BF16),
        w_out_a=b(w_out[:heads * vd]), w_out_s=b(w_out[heads * vd:]),
        g_ple=p["g_ple"][i][None], w_ple_gate=b(p["w_ple_gate"][i]), w_ple_proj=b(p["w_ple_proj"][i]),
    )


def _trunk_layer(x, pe, cos, sin, conv_prev, state_prev, attn, w, g_final, *, bsz, chunk, l_real, final):
    n, d = x.shape
    t = n // bsz
    scale = (w["nope"] + w["rope"]) ** -0.5
    x = _ffn(x, w["g_ffn1"], w["w1_a"], w["w3_a"], w["w2_a"])
    o_att, ckv, kr = attn(x, cos, sin, scale)
    z, xbc, dt = _mix_ssm(x, w["g_mix"], w)
    seq = lambda v: v.reshape(bsz, t, v.shape[1])
    if t % chunk:
        pad = lambda v: jnp.pad(seq(v), ((0, 0), (0, chunk - t), (0, 0)))
    else:
        pad = seq
    o_ssm, conv_new, ssm_new = _ssd(pad(xbc), pad(z), pad(dt), conv_prev, state_prev, w, chunk=chunk, l_real=l_real)
    o_ssm = o_ssm[:, :t].reshape(n, -1)
    x = _out_proj(x, o_att, o_ssm, w["w_out_a"], w["w_out_s"])
    x = _ffn(x, w["g_ffn2"], w["w1_b"], w["w3_b"], w["w2_b"])
    x = _ple(x, pe, w["g_ple"], w["w_ple_gate"], w["w_ple_proj"], g_final, final)
    return x, ckv, kr, conv_new[:, CONV_PAD - (CONV_WIDTH - 1):], ssm_new


def kernel(x_prompt, x_sample, p_prompt, p_sample, cache_ckv, cache_krope, state_conv, state_ssm, page_table, g_ffn1, w1_a, w3_a, w2_a, g_mix, w_in, g_kv, w_uk, w_uv, conv_w, conv_b, dt_bias, a_log, d_skip, g_ssm, w_out, g_ffn2, w1_b, w3_b, w2_b, g_ple, w_ple_gate, w_ple_proj, g_final):
    params = dict(cache_krope=cache_krope, state_ssm=state_ssm, g_ffn1=g_ffn1, w1_a=w1_a, w3_a=w3_a, w2_a=w2_a,
                  g_mix=g_mix, w_in=w_in, g_kv=g_kv, w_uk=w_uk, w_uv=w_uv, conv_w=conv_w, conv_b=conv_b,
                  dt_bias=dt_bias, a_log=a_log, d_skip=d_skip, g_ssm=g_ssm, w_out=w_out, g_ffn2=g_ffn2, w1_b=w1_b,
                  w3_b=w3_b, w2_b=w2_b, g_ple=g_ple, w_ple_gate=w_ple_gate, w_ple_proj=w_ple_proj)
    depth = w_in.shape[0]
    b_p, s_p, d = x_prompt.shape
    b_s, t_s, _ = x_sample.shape
    heads, rope = w_uk.shape[2], cache_krope.shape[3]
    lat = w_uk.shape[1]
    ssm_heads, headdim, d_state = state_ssm.shape[2:]
    d_inner = ssm_heads * headdim
    conv_dim = conv_w.shape[2]
    past_len = page_table.shape[1] * cache_ckv.shape[2]
    cos_p, sin_p = _rope_tables(jnp.arange(s_p), rope)
    cos_s, sin_s = (jnp.tile(t, (b_s, heads)) for t in _rope_tables(past_len + jnp.arange(t_s), rope))
    chunk_p = math.gcd(s_p, SSD_CHUNK)
    chunk_s = -(-t_s // BF16_ROWS) * BF16_ROWS
    g_fin = g_final[None]
    cache_krope_t = jnp.swapaxes(cache_krope, 2, 3)
    hp = x_prompt.reshape(b_p * s_p, d)
    hs = x_sample.reshape(b_s * t_s, d)
    outs = [[] for _ in range(8)]
    for i in range(depth):
        w = _prep_layer(i, params)
        final = i == depth - 1

        def attn_p(x, cos, sin, scale):
            q3t, k3, v3t, ckv, kr = _mix_attn_prompt(x, w["g_mix"], cos, sin, w, scale * LOG2_E)
            return _attn_prompt(q3t, k3, v3t, b_p), ckv, kr

        def attn_s(x, cos, sin, scale):
            ql, qr, ckv, kr = _mix_attn_sample(x, w["g_mix"], cos, sin, w, scale)
            new = lambda v: jnp.pad(v.reshape(b_s, t_s, -1), ((0, 0), (0, chunk_s - t_s), (0, 0))).astype(BF16)
            ol = _attn_decode(i, page_table, ql.reshape(b_s, t_s * heads, lat), qr.reshape(b_s, t_s * heads, rope),
                              new(ckv), new(kr), cache_ckv, cache_krope_t, heads)
            return _uv_proj(ol.reshape(b_s * t_s, heads * lat), w["w_uv_h"]), ckv, kr

        conv0 = jnp.zeros((b_p, CONV_PAD, conv_dim), F32)
        ssm0 = jnp.zeros((b_p, d_inner, d_state), F32)
        hp, c1, k1, cv1, s1 = _trunk_layer(hp, p_prompt[i].reshape(b_p * s_p, -1), cos_p, sin_p, conv0, ssm0, attn_p, w,
                                           g_fin, bsz=b_p, chunk=chunk_p, l_real=chunk_p, final=final)
        conv_s = jnp.pad(state_conv[i], ((0, 0), (CONV_PAD - (CONV_WIDTH - 1), 0), (0, 0)))
        hs, c2, k2, cv2, s2 = _trunk_layer(hs, p_sample[i].reshape(b_s * t_s, -1), cos_s, sin_s, conv_s,
                                           state_ssm[i].reshape(b_s, d_inner, d_state), attn_s, w, g_fin,
                                           bsz=b_s, chunk=chunk_s, l_real=t_s, final=final)
        for lst, v in zip(outs, (c1.reshape(b_p, s_p, -1), k1.reshape(b_p, s_p, -1), cv1,
                                 s1.reshape(b_p, ssm_heads, headdim, d_state),
                                 c2.reshape(b_s, t_s, -1), k2.reshape(b_s, t_s, -1), cv2,
                                 s2.reshape(b_s, ssm_heads, headdim, d_state))):
            lst.append(v)
    return (hp.reshape(b_p, s_p, d), hs.reshape(b_s, t_s, d)) + tuple(jnp.stack(o) for o in outs)
```

```python
import functools
import math

import jax
import jax.numpy as jnp
from jax import lax
from jax.experimental import pallas as pl
from jax.experimental.pallas import tpu as pltpu

F32 = jnp.float32
BF16 = jnp.bfloat16

EPS = 1e-6
HALF_STEP = 0.5
ROPE_THETA = 10000.0
SSD_CHUNK = 128
CONV_WIDTH = 4
LANES = 128
BF16_ROWS = 16
CONV_PAD = 8
V7X_VMEM_LIMIT = 56 * 1024 * 1024

FFN_ROWS = 1024
FFN_COLS = 256
MIX_ROWS = 256
ROW_TILE = 512
ATTN_Q = 1024
ATTN_K = 1024
ATTN_HEADS = 2
LOG2_E = math.log2(math.e)
DECODE_PAGES = 32
DECODE_GROUP = 8
DECODE_SLOTS = 3
SSD_SEQS = 4

_NT = (((1,), (1,)), ((), ()))
_TN = (((0,), (0,)), ((), ()))


def _params(*sem):
    return pltpu.CompilerParams(dimension_semantics=sem, vmem_limit_bytes=V7X_VMEM_LIMIT)


def _rms(x, g):
    return x * lax.rsqrt(jnp.mean(x * x, axis=-1, keepdims=True) + EPS) * g


def _dot(a, b):
    return jnp.dot(a, b, preferred_element_type=F32)


def _dot_t(a, b):
    return lax.dot_general(a, b, _NT, preferred_element_type=F32)


def _split3(v):
    hi = v.astype(BF16)
    r1 = v - hi.astype(F32)
    mid = r1.astype(BF16)
    lo = (r1 - mid.astype(F32)).astype(BF16)
    return hi, mid, lo


def _exact_dot(a, b, dims=None, split="rhs"):
    f = (lambda x, y: _dot(x, y)) if dims is None else (lambda x, y: lax.dot_general(x, y, dims, preferred_element_type=F32))
    if split == "rhs":
        h, m, l = _split3(b)
        return f(a, h) + f(a, m) + f(a, l)
    h, m, l = _split3(a)
    return f(h, b) + f(m, b) + f(l, b)


def _ffn_body(x_ref, g_ref, w1_ref, w3_ref, w2_ref, o_ref, xn_ref):
    @pl.when(pl.program_id(1) == 0)
    def _():
        x = x_ref[...]
        xn_ref[...] = _rms(x, g_ref[...]).astype(BF16)
        o_ref[...] = x

    xn = xn_ref[...]
    h1 = _dot(xn, w1_ref[...].astype(BF16))
    h3 = _dot(xn, w3_ref[...].astype(BF16))
    act = (h1 * jax.nn.sigmoid(h1) * h3).astype(BF16)
    o_ref[...] += HALF_STEP * _dot(act, w2_ref[...].astype(BF16))


def _ffn(x, g, w1, w3, w2):
    n, d = x.shape
    f = w1.shape[1]
    tm = min(FFN_ROWS, n)
    tf = FFN_COLS * (FFN_ROWS // tm)
    tf = tf if f % tf == 0 else FFN_COLS
    return pl.pallas_call(
        _ffn_body,
        out_shape=jax.ShapeDtypeStruct((n, d), F32),
        grid=(n // tm, f // tf),
        in_specs=[
            pl.BlockSpec((tm, d), lambda i, j: (i, 0)),
            pl.BlockSpec((1, d), lambda i, j: (0, 0)),
            pl.BlockSpec((d, tf), lambda i, j: (0, j)),
            pl.BlockSpec((d, tf), lambda i, j: (0, j)),
            pl.BlockSpec((tf, d), lambda i, j: (j, 0)),
        ],
        out_specs=pl.BlockSpec((tm, d), lambda i, j: (i, 0)),
        scratch_shapes=[pltpu.VMEM((tm, d), BF16)],
        compiler_params=_params("parallel", "arbitrary"),
        name="ffn_half_step",
    )(x, g, w1, w3, w2)


def _latent_and_rope_key(h, cos, sin, wckv_ref, wkr_ref, wkrr_ref, gkv_ref):
    ckv = _rms(_dot_t(h, wckv_ref[...]), gkv_ref[...])
    kr = _dot_t(h, wkr_ref[...]) * cos + _dot_t(h, wkrr_ref[...]) * sin
    return ckv, kr


def _mix_attn_prompt_body(x_ref, g_ref, cos_ref, sin_ref, cost_ref, sint_ref, wqnt_ref, wqrt_ref, wqrrt_ref,
                          wckv_ref, wkr_ref, wkrr_ref, gkv_ref, wuk_ref, wuvt_ref,
                          qt_ref, k3_ref, vt_ref, ckv_ref, kr_ref, *, heads, nope, rope, scale):
    h = _rms(x_ref[...], g_ref[...]).astype(BF16)
    qn_t = _dot_t(wqnt_ref[...], h) * scale
    qr_t = (_dot_t(wqrt_ref[...], h) * cost_ref[...] + _dot_t(wqrrt_ref[...], h) * sint_ref[...]) * scale
    ckv, kr = _latent_and_rope_key(h, cos_ref[...], sin_ref[...], wckv_ref, wkr_ref, wkrr_ref, gkv_ref)
    ckv_b = ckv.astype(BF16)
    kn = _dot(ckv_b, wuk_ref[...])
    v_t = _dot_t(wuvt_ref[...], ckv_b)
    kr_b = kr.astype(BF16)
    vd = v_t.shape[0] // heads
    for hd in range(heads):
        qt_ref[hd, 0:nope, :] = qn_t[hd * nope:(hd + 1) * nope, :].astype(BF16)
        qt_ref[hd, nope:nope + rope, :] = qr_t[hd * rope:(hd + 1) * rope, :].astype(BF16)
        k3_ref[hd, :, 0:nope] = kn[:, hd * nope:(hd + 1) * nope].astype(BF16)
        k3_ref[hd, :, nope:nope + rope] = kr_b
        vt_ref[hd] = v_t[hd * vd:(hd + 1) * vd, :].astype(BF16)
    ckv_ref[...] = ckv
    kr_ref[...] = kr


def _mix_attn_sample_body(x_ref, g_ref, cos_ref, sin_ref, wqn_ref, wqr_ref, wqrr_ref, wckv_ref, wkr_ref, wkrr_ref,
                          gkv_ref, wukt_ref, ql_ref, qr_ref, ckv_ref, kr_ref, *, heads, nope, rope, scale):
    h = _rms(x_ref[...], g_ref[...]).astype(BF16)
    cos = cos_ref[...]
    sin = sin_ref[...]
    qn = _dot_t(h, wqn_ref[...]) * scale
    qr = (_dot_t(h, wqr_ref[...]) * cos + _dot_t(h, wqrr_ref[...]) * sin) * scale
    ckv, kr = _latent_and_rope_key(h, cos[:, :rope], sin[:, :rope], wckv_ref, wkr_ref, wkrr_ref, gkv_ref)
    lat = wukt_ref.shape[2]
    for hd in range(heads):
        qh = qn[:, hd * nope:(hd + 1) * nope].astype(BF16)
        ql_ref[:, hd * lat:(hd + 1) * lat] = _dot(qh, wukt_ref[hd]).astype(BF16)
    qr_ref[...] = qr.astype(BF16)
    ckv_ref[...] = ckv
    kr_ref[...] = kr


def _mix_attn_prompt(x, g, cos, sin, w, scale):
    n, d = x.shape
    heads, nope, rope, lat = w["heads"], w["nope"], w["rope"], w["lat"]
    tm = min(MIX_ROWS, n)
    period = cos.shape[0] // tm
    cos_t, sin_t = jnp.tile(cos, (1, heads)).T, jnp.tile(sin, (1, heads)).T
    vd = w["w_uv_t"].shape[0] // heads
    const = lambda i: (0, 0)
    row = lambda i: (i, 0)
    ws = [w["w_qn_t"], w["w_qr_t"], w["w_qr_rot_t"], w["w_ckv_t"], w["w_kr_t"], w["w_kr_rot_t"], w["g_kv"],
          w["w_uk"], w["w_uv_t"]]
    body = functools.partial(_mix_attn_prompt_body, heads=heads, nope=nope, rope=rope, scale=scale)
    return pl.pallas_call(
        body,
        out_shape=[jax.ShapeDtypeStruct((heads, nope + rope, n), BF16),
                   jax.ShapeDtypeStruct((heads, n, nope + rope), BF16),
                   jax.ShapeDtypeStruct((heads, vd, n), BF16),
                   jax.ShapeDtypeStruct((n, lat), F32),
                   jax.ShapeDtypeStruct((n, rope), F32)],
        grid=(n // tm,),
        in_specs=[
            pl.BlockSpec((tm, d), row),
            pl.BlockSpec((1, d), const),
            pl.BlockSpec((tm, rope), lambda i: (i % period, 0)),
            pl.BlockSpec((tm, rope), lambda i: (i % period, 0)),
            pl.BlockSpec((heads * rope, tm), lambda i: (0, i % period)),
            pl.BlockSpec((heads * rope, tm), lambda i: (0, i % period)),
        ] + [pl.BlockSpec(v.shape, const) for v in ws],
        out_specs=[pl.BlockSpec((heads, nope + rope, tm), lambda i: (0, 0, i)),
                   pl.BlockSpec((heads, tm, nope + rope), lambda i: (0, i, 0)),
                   pl.BlockSpec((heads, vd, tm), lambda i: (0, 0, i)),
                   pl.BlockSpec((tm, lat), row),
                   pl.BlockSpec((tm, rope), row)],
        compiler_params=_params("parallel"),
        name="mix_attn_prompt",
    )(x, g, cos, sin, cos_t, sin_t, *ws)


def _mix_attn_sample(x, g, cos, sin, w, scale):
    n, d = x.shape
    heads, nope, rope, lat = w["heads"], w["nope"], w["rope"], w["lat"]
    tm = min(MIX_ROWS, n)
    const = lambda i: (0, 0)
    row = lambda i: (i, 0)
    ws = [w["w_qn_t"], w["w_qr_t"], w["w_qr_rot_t"], w["w_ckv_t"], w["w_kr_t"], w["w_kr_rot_t"], w["g_kv"]]
    body = functools.partial(_mix_attn_sample_body, heads=heads, nope=nope, rope=rope, scale=scale)
    return pl.pallas_call(
        body,
        out_shape=[jax.ShapeDtypeStruct((n, heads * lat), BF16),
                   jax.ShapeDtypeStruct((n, heads * rope), BF16),
                   jax.ShapeDtypeStruct((n, lat), F32),
                   jax.ShapeDtypeStruct((n, rope), F32)],
        grid=(n // tm,),
        in_specs=[pl.BlockSpec((tm, d), row), pl.BlockSpec((1, d), const),
                  pl.BlockSpec((tm, heads * rope), row), pl.BlockSpec((tm, heads * rope), row)]
                 + [pl.BlockSpec(v.shape, const) for v in ws]
                 + [pl.BlockSpec(w["w_uk_t"].shape, lambda i: (0, 0, 0))],
        out_specs=[pl.BlockSpec((tm, heads * lat), row), pl.BlockSpec((tm, heads * rope), row),
                   pl.BlockSpec((tm, lat), row), pl.BlockSpec((tm, rope), row)],
        compiler_params=_params("parallel"),
        name="mix_attn_sample",
    )(x, g, cos, sin, *ws, w["w_uk_t"])


def _mix_ssm_body(x_ref, g_ref, wz_ref, wx_ref, wdt_ref, z_ref, xbc_ref, dt_ref):
    h = _rms(x_ref[...], g_ref[...]).astype(BF16)
    z_ref[...] = _dot_t(h, wz_ref[...])
    xbc_ref[...] = _dot_t(h, wx_ref[...])
    dt_ref[...] = _dot_t(h, wdt_ref[...])


def _mix_ssm(x, g, w):
    n, d = x.shape
    tm = min(MIX_ROWS, n)
    const = lambda i: (0, 0)
    row = lambda i: (i, 0)
    ws = [w["w_z_t"], w["w_xbc_t"], w["w_dt_t"]]
    return pl.pallas_call(
        _mix_ssm_body,
        out_shape=[jax.ShapeDtypeStruct((n, wi.shape[0]), F32) for wi in ws],
        grid=(n // tm,),
        in_specs=[pl.BlockSpec((tm, d), row), pl.BlockSpec((1, d), const)] + [pl.BlockSpec(wi.shape, const) for wi in ws],
        out_specs=[pl.BlockSpec((tm, wi.shape[0]), row) for wi in ws],
        compiler_params=_params("parallel"),
        name="mix_ssm",
    )(x, g, *ws)


def _softmax_step_t(q_t, kb, vb_t, carry, mask):
    m, l, acc_t = carry
    s_t = _dot(kb, q_t)
    if mask is not None:
        s_t = jnp.where(mask, s_t, -jnp.inf)
    m_new = jnp.maximum(m, jnp.max(s_t, axis=0, keepdims=True))
    alpha = jnp.exp2(m - m_new)
    p_t = jnp.exp2(s_t - m_new)
    l = alpha * l + jnp.sum(p_t, axis=0, keepdims=True)
    acc_t = alpha * acc_t + _dot(vb_t, p_t.astype(BF16))
    return m_new, l, acc_t


def _attn_body(qt_ref, k_ref, vt_ref, o_ref, *, tq, tk, nh):
    i = pl.program_id(2)
    vd = vt_ref.shape[1]
    qts = [qt_ref[h] for h in range(nh)]

    def blocks(off, carries, mask):
        return tuple(_softmax_step_t(qts[h], k_ref[h, pl.ds(off, tk), :], vt_ref[h, :, pl.ds(off, tk)], carries[h], mask)
                     for h in range(nh))

    def full_block(j, carries):
        return blocks(pl.multiple_of(j * tk, tk), carries, None)

    carries = tuple((jnp.full((1, tq), -jnp.inf, F32), jnp.zeros((1, tq), F32), jnp.zeros((vd, tq), F32))
                    for _ in range(nh))
    carries = lax.fori_loop(0, i * (tq // tk), full_block, carries)
    keys = lax.broadcasted_iota(jnp.int32, (tk, tq), 0)
    queries = lax.broadcasted_iota(jnp.int32, (tk, tq), 1)
    for dblk in range(tq // tk):
        carries = blocks(pl.multiple_of(i * tq + dblk * tk, tk), carries, keys + dblk * tk <= queries)
    for h, (_, l, acc_t) in enumerate(carries):
        o_ref[:, h * vd:(h + 1) * vd] = (acc_t / l).T.astype(BF16)


def _attn_prompt(q3t, k3, v3t, batch):
    heads, dq, n = q3t.shape
    vd = v3t.shape[1]
    s = n // batch
    tq = min(ATTN_Q, s)
    tk = min(ATTN_K, tq)
    nq = s // tq
    nh = math.gcd(ATTN_HEADS, heads)
    return pl.pallas_call(
        functools.partial(_attn_body, tq=tq, tk=tk, nh=nh),
        out_shape=jax.ShapeDtypeStruct((n, heads * vd), BF16),
        grid=(batch, heads // nh, nq),
        in_specs=[
            pl.BlockSpec((nh, dq, tq), lambda b, h, i: (h, 0, b * nq + i)),
            pl.BlockSpec((nh, s, dq), lambda b, h, i: (h, b, 0)),
            pl.BlockSpec((nh, vd, s), lambda b, h, i: (h, 0, b)),
        ],
        out_specs=pl.BlockSpec((tq, nh * vd), lambda b, h, i: (b * nq + i, h)),
        compiler_params=_params("parallel", "parallel", "arbitrary"),
        name="attn_prompt",
    )(q3t, k3, v3t)


def _decode_body(pt_ref, ql_ref, qr_ref, cn_ref, kn_ref, ckv_hbm, krt_hbm, o_ref,
                 ckv_buf, krt_buf, sem, m_ref, l_ref, acc_ref, *, layer, pages, group, heads):
    b, p = pl.program_id(0), pl.program_id(1)
    nb, ns = pl.num_programs(0), pl.num_programs(1)
    ring = ckv_buf.shape[0]
    total = nb * ns
    t = b * ns + p
    slot = lax.rem(t, ring)
    page = ckv_buf.shape[1] // pages

    def step_copies(ahead, k):
        tn = t + ahead
        tn = jnp.where(tn >= total, tn - total, tn)
        pid = pt_ref[lax.div(tn, ns), lax.rem(tn, ns) * pages + k]
        sl = lax.rem(t + ahead, ring)
        rows = pl.ds(k * page, page)
        return (pltpu.make_async_copy(ckv_hbm.at[layer, pid], ckv_buf.at[sl, rows, :], sem.at[sl, 0]),
                pltpu.make_async_copy(krt_hbm.at[layer, pid], krt_buf.at[sl, :, rows], sem.at[sl, 1]))

    @pl.when(t == 0)
    def _():
        for ahead in range(ring - 1):
            for k in range(pages):
                for c in step_copies(ahead, k):
                    c.start()

    for k in range(pages):
        for c in step_copies(0, k):
            c.wait()

    @pl.when(p == 0)
    def _():
        m_ref[...] = jnp.full_like(m_ref, -jnp.inf)
        l_ref[...] = jnp.zeros_like(l_ref)
        acc_ref[...] = jnp.zeros_like(acc_ref)

    last_p = p == ns - 1
    ql = ql_ref[0]
    qr = qr_ref[0]

    def update(s, kv):
        m = m_ref[...]
        m_new = jnp.maximum(m, jnp.max(s, axis=-1, keepdims=True))
        alpha = jnp.exp(m - m_new)
        pr = jnp.exp(s - m_new)
        l_ref[...] = alpha * l_ref[...] + jnp.sum(pr, axis=-1, keepdims=True)
        acc_ref[...] = alpha * acc_ref[...] + _dot(pr.astype(BF16), kv)
        m_ref[...] = m_new

    s_parts, kv_parts = [], []
    for j in range(pages // group):
        for k in range(j * group, (j + 1) * group):
            for c in step_copies(ring - 1, k):
                c.start()
        rows = pl.ds(j * group * page, group * page)
        kv_j = ckv_buf[slot, rows, :].astype(BF16)
        s_parts.append(lax.dot_general(ql, kv_j, _NT, preferred_element_type=F32)
                       + _dot(qr, krt_buf[slot, :, rows].astype(BF16)))
        kv_parts.append(kv_j)
    update(jnp.concatenate(s_parts, axis=1), jnp.concatenate(kv_parts, axis=0))

    @pl.when(last_p)
    def _():
        rows, nk = ql.shape[0], cn_ref.shape[1]
        qtok = lax.broadcasted_iota(jnp.int32, (rows, nk), 0) // heads
        ktok = lax.broadcasted_iota(jnp.int32, (rows, nk), 1)
        kvn = cn_ref[0]
        s = (lax.dot_general(ql, kvn, _NT, preferred_element_type=F32)
             + lax.dot_general(qr, kn_ref[0], _NT, preferred_element_type=F32))
        update(jnp.where(ktok <= qtok, s, -jnp.inf), kvn)
        o_ref[0] = (acc_ref[...] / l_ref[...]).astype(BF16)

    @pl.when(t == total - 1)
    def _():
        for ahead in range(1, ring):
            for k in range(pages):
                for c in step_copies(ahead, k):
                    c.wait()


def _attn_decode(layer, page_table, ql, qr, ckv_new, kr_new, cache_ckv, cache_krope_t, heads):
    bsz, rows, lat = ql.shape
    rope = qr.shape[2]
    n_pages = page_table.shape[1]
    page = cache_ckv.shape[2]
    g = math.gcd(DECODE_PAGES, n_pages)
    nk = ckv_new.shape[1]
    bmap = lambda b, p, pt: (b, 0, 0)
    grid_spec = pltpu.PrefetchScalarGridSpec(
        num_scalar_prefetch=1,
        grid=(bsz, n_pages // g),
        in_specs=[
            pl.BlockSpec((1, rows, lat), bmap),
            pl.BlockSpec((1, rows, rope), bmap),
            pl.BlockSpec((1, nk, lat), bmap),
            pl.BlockSpec((1, nk, rope), bmap),
            pl.BlockSpec(memory_space=pl.ANY),
            pl.BlockSpec(memory_space=pl.ANY),
        ],
        out_specs=pl.BlockSpec((1, rows, lat), bmap),
        scratch_shapes=[pltpu.VMEM((DECODE_SLOTS, g * page, lat), F32), pltpu.VMEM((DECODE_SLOTS, rope, g * page), F32),
                        pltpu.SemaphoreType.DMA((DECODE_SLOTS, 2)),
                        pltpu.VMEM((rows, 1), F32), pltpu.VMEM((rows, 1), F32), pltpu.VMEM((rows, lat), F32)],
    )
    return pl.pallas_call(
        functools.partial(_decode_body, layer=layer, pages=g, group=math.gcd(DECODE_GROUP, g), heads=heads),
        grid_spec=grid_spec,
        out_shape=jax.ShapeDtypeStruct((bsz, rows, lat), BF16),
        compiler_params=_params("arbitrary", "arbitrary"),
        name="attn_decode",
    )(page_table, ql, qr, ckv_new, kr_new, cache_ckv, cache_krope_t)


def _uv_body(ol_ref, wuv_ref, o_ref, *, heads):
    lat, vd = wuv_ref.shape[1], wuv_ref.shape[2]
    for hd in range(heads):
        o_ref[:, hd * vd:(hd + 1) * vd] = _dot(ol_ref[:, hd * lat:(hd + 1) * lat], wuv_ref[hd]).astype(BF16)


def _uv_proj(ol, wuv_h):
    n = ol.shape[0]
    heads, lat, vd = wuv_h.shape
    tm = min(ROW_TILE, n)
    return pl.pallas_call(
        functools.partial(_uv_body, heads=heads),
        out_shape=jax.ShapeDtypeStruct((n, heads * vd), BF16),
        grid=(n // tm,),
        in_specs=[pl.BlockSpec((tm, heads * lat), lambda i: (i, 0)), pl.BlockSpec(wuv_h.shape, lambda i: (0, 0, 0))],
        out_specs=pl.BlockSpec((tm, heads * vd), lambda i: (i, 0)),
        compiler_params=_params("parallel"),
        name="uv_proj",
    )(ol, wuv_h)


def _ssd_body(xbc_ref, z_ref, dt_ref, cprev_ref, sprev_ref, cw_ref, cb_ref, dtb_ref, alog_ref, dsk_ref, g_ref, r_ref,
              y_ref, cnew_ref, snew_ref, xp_ref, *, nseq, **dims):
    for sq in range(nseq):
        _ssd_seq(sq, xbc_ref, z_ref, dt_ref, cprev_ref, sprev_ref, cw_ref, cb_ref, dtb_ref, alog_ref, dsk_ref, g_ref,
                 r_ref, y_ref, cnew_ref, snew_ref, xp_ref, **dims)


def _ssd_seq(sq, xbc_ref, z_ref, dt_ref, cprev_ref, sprev_ref, cw_ref, cb_ref, dtb_ref, alog_ref, dsk_ref, g_ref, r_ref,
             y_ref, cnew_ref, snew_ref, xp_ref, *, chunk, l_real, heads, headdim, groups, d_state, single_chunk):
    c = pl.program_id(1)
    last = pl.num_programs(1) - 1
    d_inner = heads * headdim
    gw = d_inner // groups
    hpg = heads // groups
    state_in = sprev_ref if single_chunk else snew_ref

    @pl.when(c == 0)
    def _():
        if not single_chunk:
            snew_ref[sq] = sprev_ref[sq]
        xp_ref[sq, 0:CONV_PAD, :] = cprev_ref[sq]

    xp_ref[sq, CONV_PAD:CONV_PAD + chunk, :] = xbc_ref[sq]
    cw = cw_ref[...]
    y = cb_ref[...]
    for k in range(CONV_WIDTH):
        y = y + xp_ref[sq, CONV_PAD - (CONV_WIDTH - 1) + k:CONV_PAD - (CONV_WIDTH - 1) + k + chunk, :] * cw[k:k + 1, :]
    xbc = y * jax.nn.sigmoid(y)
    tail = xp_ref[sq, l_real:l_real + CONV_PAD, :]
    xp_ref[sq, 0:CONV_PAD, :] = tail

    @pl.when(c == last)
    def _():
        cnew_ref[sq] = tail

    xs = xbc[:, :d_inner]
    bm = xbc[:, d_inner:d_inner + groups * d_state].astype(BF16)
    cm = xbc[:, d_inner + groups * d_state:].astype(BF16)

    t_row = lax.broadcasted_iota(jnp.int32, (chunk, LANES), 0)
    dtr = dt_ref[sq] + dtb_ref[...]
    dt = jnp.maximum(dtr, 0.0) + jnp.log1p(jnp.exp(-jnp.abs(dtr)))
    if l_real < chunk:
        dt = jnp.where(t_row < l_real, dt, 0.0)
    a = dt * (-jnp.exp(alog_ref[...]))
    tri = (lax.broadcasted_iota(jnp.int32, (chunk, chunk), 0) >= lax.broadcasted_iota(jnp.int32, (chunk, chunk), 1))
    cs = _exact_dot(jnp.where(tri, 1.0, 0.0).astype(BF16), a)
    cs_last = cs[chunk - 1:chunk, :]
    ecs = jnp.exp(cs)
    dec = jnp.exp(cs_last - cs)
    expanded = _exact_dot(jnp.concatenate([dt, ecs, dec], axis=0), r_ref[...], split="lhs")
    dt_e = expanded[0:chunk]
    ecs_e = expanded[chunk:2 * chunk]
    dec_e = expanded[2 * chunk:3 * chunk]
    xdt = xs * dt_e

    eye = (lax.broadcasted_iota(jnp.int32, (LANES, LANES), 0) == lax.broadcasted_iota(jnp.int32, (LANES, LANES), 1))
    cs_t = _exact_dot(jnp.where(eye, 1.0, 0.0).astype(BF16), cs, dims=_NT)
    xdt_p = xdt.astype(BF16)
    bm_p = bm
    causal = lax.broadcasted_iota(jnp.int32, (chunk, chunk), 1) <= lax.broadcasted_iota(jnp.int32, (chunk, chunk), 0)
    ecl = jnp.exp(cs_last)

    y_groups = []
    for g in range(groups):
        cg = cm[:, g * d_state:(g + 1) * d_state]
        cb_mat = lax.dot_general(cg, bm_p[:, g * d_state:(g + 1) * d_state], _NT, preferred_element_type=F32)
        sg = state_in[sq, g * gw:(g + 1) * gw, :]
        y_off = lax.dot_general(cg, sg.astype(BF16), _NT, preferred_element_type=F32)
        y_heads = []
        for hh in range(hpg):
            hd = g * hpg + hh
            seg = jnp.exp(jnp.where(causal, cs[:, hd:hd + 1] - cs_t[hd:hd + 1, :], -jnp.inf))
            y_heads.append(_dot((cb_mat * seg).astype(BF16), xdt_p[:, hd * headdim:(hd + 1) * headdim]))
        y_groups.append(jnp.concatenate(y_heads, axis=1) + y_off * ecs_e[:, g * gw:(g + 1) * gw])
        xd = (xdt[:, g * gw:(g + 1) * gw] * dec_e[:, g * gw:(g + 1) * gw]).astype(BF16)
        upd = lax.dot_general(xd, bm[:, g * d_state:(g + 1) * d_state], _TN, preferred_element_type=F32)
        for hh in range(hpg):
            hd = g * hpg + hh
            r0 = g * gw + hh * headdim
            snew_ref[sq, r0:r0 + headdim, :] = (sg[hh * headdim:(hh + 1) * headdim, :] * ecl[0:1, hd:hd + 1]
                                                + upd[hh * headdim:(hh + 1) * headdim, :])

    yv = jnp.concatenate(y_groups, axis=1) + xs * dsk_ref[...]
    zv = z_ref[sq]
    yv = yv * (zv * jax.nn.sigmoid(zv))
    gv = g_ref[...]
    outs = [_rms(yv[:, g * gw:(g + 1) * gw], gv[:, g * gw:(g + 1) * gw]) for g in range(groups)]
    y_ref[sq] = jnp.concatenate(outs, axis=1).astype(BF16)


def _ssd(xbc, z, dt, conv_prev, state_prev, w, *, chunk, l_real):
    bsz, t, cd = xbc.shape
    heads, headdim, groups, d_state = w["ssm_heads"], w["ssm_headdim"], w["ssm_groups"], w["ssm_state"]
    d_inner = heads * headdim
    nc = t // chunk
    nseq = math.gcd(SSD_SEQS, bsz)
    tok = lambda b, c: (b, c, 0)
    seq = lambda b, c: (b, 0, 0)
    const = lambda b, c: (0, 0)
    consts = [w["conv_w"], w["conv_b"], w["dt_bias"], w["a_log"], w["d_skip"], w["g_ssm"], w["head_expand"]]
    body = functools.partial(_ssd_body, nseq=nseq, chunk=chunk, l_real=l_real, heads=heads, headdim=headdim,
                             groups=groups, d_state=d_state, single_chunk=nc == 1)
    return pl.pallas_call(
        body,
        out_shape=[jax.ShapeDtypeStruct((bsz, t, d_inner), BF16),
                   jax.ShapeDtypeStruct((bsz, CONV_PAD, cd), F32),
                   jax.ShapeDtypeStruct((bsz, d_inner, d_state), F32)],
        grid=(bsz // nseq, nc),
        in_specs=[
            pl.BlockSpec((nseq, chunk, cd), tok),
            pl.BlockSpec((nseq, chunk, d_inner), tok),
            pl.BlockSpec((nseq, chunk, LANES), tok),
            pl.BlockSpec((nseq, CONV_PAD, cd), seq),
            pl.BlockSpec((nseq, d_inner, d_state), seq),
        ] + [pl.BlockSpec(v.shape, const) for v in consts],
        out_specs=[pl.BlockSpec((nseq, chunk, d_inner), tok),
                   pl.BlockSpec((nseq, CONV_PAD, cd), seq),
                   pl.BlockSpec((nseq, d_inner, d_state), seq)],
        scratch_shapes=[pltpu.VMEM((nseq, CONV_PAD + chunk, cd), F32)],
        compiler_params=_params("parallel", "arbitrary"),
        name="ssd_mixer",
    )(xbc, z, dt, conv_prev, state_prev, *consts)


def _out_proj_body(x_ref, oa_ref, os_ref, wa_ref, ws_ref, o_ref):
    o_ref[...] = x_ref[...] + _dot(oa_ref[...], wa_ref[...]) + _dot(os_ref[...], ws_ref[...])


def _out_proj(x, oa, osm, wa, ws):
    n, d = x.shape
    tm = min(ROW_TILE, n)
    row = lambda i: (i, 0)
    const = lambda i: (0, 0)
    return pl.pallas_call(
        _out_proj_body,
        out_shape=jax.ShapeDtypeStruct((n, d), F32),
        grid=(n // tm,),
        in_specs=[pl.BlockSpec((tm, d), row), pl.BlockSpec((tm, oa.shape[1]), row), pl.BlockSpec((tm, osm.shape[1]), row),
                  pl.BlockSpec(wa.shape, const), pl.BlockSpec(ws.shape, const)],
        out_specs=pl.BlockSpec((tm, d), row),
        compiler_params=_params("parallel"),
        name="out_proj",
    )(x, oa, osm, wa, ws)


def _ple_body(x_ref, pe_ref, g_ref, wg_ref, wp_ref, gf_ref, o_ref, *, final):
    x = x_ref[...]
    gate = jax.nn.sigmoid(_dot(_rms(x, g_ref[...]).astype(BF16), wg_ref[...]))
    x = x + gate * _dot(pe_ref[...].astype(BF16), wp_ref[...])
    o_ref[...] = _rms(x, gf_ref[...]) if final else x


def _ple(x, pe, g, wg, wp, g_final, final):
    n, d = x.shape
    tm = min(ROW_TILE, n)
    row = lambda i: (i, 0)
    const = lambda i: (0, 0)
    return pl.pallas_call(
        functools.partial(_ple_body, final=final),
        out_shape=jax.ShapeDtypeStruct((n, d), F32),
        grid=(n // tm,),
        in_specs=[pl.BlockSpec((tm, d), row), pl.BlockSpec((tm, pe.shape[1]), row), pl.BlockSpec((1, d), const),
                  pl.BlockSpec(wg.shape, const), pl.BlockSpec(wp.shape, const), pl.BlockSpec((1, d), const)],
        out_specs=pl.BlockSpec((tm, d), row),
        compiler_params=_params("parallel"),
        name="ple_final",
    )(x, pe, g, wg, wp, g_final)


def _rope_tables(pos, rope):
    half = rope // 2
    inv = ROPE_THETA ** (-jnp.arange(half, dtype=F32) / half)
    ang = pos.astype(F32)[:, None] * inv[None, :]
    cos, sin = jnp.cos(ang), jnp.sin(ang)
    return jnp.concatenate([cos, cos], axis=-1), jnp.concatenate([sin, sin], axis=-1)


def _rot_rows(w_t, rope):
    d = w_t.shape[1]
    wr = w_t.reshape(-1, 2, rope // 2, d)
    return jnp.stack([-wr[:, 1], wr[:, 0]], axis=1).reshape(-1, d)


def _prep_layer(i, p):
    d = p["w_in"].shape[1]
    lat, heads, nope = p["w_uk"].shape[1:]
    vd = p["w_uv"].shape[3]
    rope = p["cache_krope"].shape[3]
    ssm_heads, headdim, d_state = p["state_ssm"].shape[2:]
    d_inner = ssm_heads * headdim
    conv_dim = p["conv_w"].shape[2]
    groups = (conv_dim - d_inner) // (2 * d_state)
    q_dim = heads * (nope + rope)
    w_in_t = jnp.swapaxes(p["w_in"], 1, 2)[i]
    o = 0
    wq_t = w_in_t[o:o + q_dim].reshape(heads, nope + rope, d); o += q_dim
    w_ckv_t = w_in_t[o:o + lat]; o += lat
    w_kr_t = w_in_t[o:o + rope]; o += rope
    w_z_t = w_in_t[o:o + d_inner]; o += d_inner
    w_xbc_t = w_in_t[o:o + conv_dim]; o += conv_dim
    w_dt_t = w_in_t[o:o + ssm_heads]
    w_qn_t = wq_t[:, :nope].reshape(heads * nope, d)
    w_qr_t = wq_t[:, nope:].reshape(heads * rope, d)
    b = lambda t: t.astype(BF16)
    pad_heads = lambda v: jnp.pad(v.reshape(1, -1), ((0, 0), (0, LANES - ssm_heads)))
    head_of_channel = jnp.arange(d_inner) // headdim
    w_out = p["w_out"][i]
    return dict(
        heads=heads, nope=nope, rope=rope, lat=lat,
        ssm_heads=ssm_heads, ssm_headdim=headdim, ssm_groups=groups, ssm_state=d_state,
        g_ffn1=p["g_ffn1"][i][None], w1_a=p["w1_a"][i], w3_a=p["w3_a"][i], w2_a=p["w2_a"][i],
        g_ffn2=p["g_ffn2"][i][None], w1_b=p["w1_b"][i], w3_b=p["w3_b"][i], w2_b=p["w2_b"][i],
        g_mix=p["g_mix"][i][None],
        w_qn_t=b(w_qn_t), w_qr_t=b(w_qr_t), w_qr_rot_t=b(_rot_rows(w_qr_t, rope)),
        w_ckv_t=b(w_ckv_t), w_kr_t=b(w_kr_t), w_kr_rot_t=b(_rot_rows(w_kr_t, rope)), g_kv=p["g_kv"][i][None],
        w_uk=b(p["w_uk"][i].reshape(lat, heads * nope)), w_uv_t=b(p["w_uv"][i].reshape(lat, heads * vd).T),
        w_uk_t=b(jnp.transpose(p["w_uk"][i], (1, 2, 0))), w_uv_h=b(jnp.transpose(p["w_uv"][i], (1, 0, 2))),
        w_z_t=b(w_z_t), w_xbc_t=b(w_xbc_t), w_dt_t=b(jnp.pad(w_dt_t, ((0, LANES - ssm_heads), (0, 0)))),
        conv_w=p["conv_w"][i], conv_b=p["conv_b"][i][None], dt_bias=pad_heads(p["dt_bias"][i]),
        a_log=pad_heads(p["a_log"][i]), d_skip=jnp.repeat(p["d_skip"][i], headdim)[None], g_ssm=p["g_ssm"][i][None],
        head_expand=(jnp.arange(LANES)[:, None] == head_of_channel[None, :]).astype(BF16),
        w_out_a=b(w_out[:heads * vd]), w_out_s=b(w_out[heads * vd:]),
        g_ple=p["g_ple"][i][None], w_ple_gate=b(p["w_ple_gate"][i]), w_ple_proj=b(p["w_ple_proj"][i]),
    )


def _trunk_layer(x, pe, cos, sin, conv_prev, state_prev, attn, w, g_final, *, bsz, chunk, l_real, final):
    n, d = x.shape
    t = n // bsz
    scale = (w["nope"] + w["rope"]) ** -0.5
    x = _ffn(x, w["g_ffn1"], w["w1_a"], w["w3_a"], w["w2_a"])
    o_att, ckv, kr = attn(x, cos, sin, scale)
    z, xbc, dt = _mix_ssm(x, w["g_mix"], w)
    seq = lambda v: v.reshape(bsz, t, v.shape[1])
    if t % chunk:
        pad = lambda v: jnp.pad(seq(v), ((0, 0), (0, chunk - t), (0, 0)))
    else:
        pad = seq
    o_ssm, conv_new, ssm_new = _ssd(pad(xbc), pad(z), pad(dt), conv_prev, state_prev, w, chunk=chunk, l_real=l_real)
    o_ssm = o_ssm[:, :t].reshape(n, -1)
    x = _out_proj(x, o_att, o_ssm, w["w_out_a"], w["w_out_s"])
    x = _ffn(x, w["g_ffn2"], w["w1_b"], w["w3_b"], w["w2_b"])
    x = _ple(x, pe, w["g_ple"], w["w_ple_gate"], w["w_ple_proj"], g_final, final)
    return x, ckv, kr, conv_new[:, CONV_PAD - (CONV_WIDTH - 1):], ssm_new


def kernel(x_prompt, x_sample, p_prompt, p_sample, cache_ckv, cache_krope, state_conv, state_ssm, page_table, g_ffn1, w1_a, w3_a, w2_a, g_mix, w_in, g_kv, w_uk, w_uv, conv_w, conv_b, dt_bias, a_log, d_skip, g_ssm, w_out, g_ffn2, w1_b, w3_b, w2_b, g_ple, w_ple_gate, w_ple_proj, g_final):
    params = dict(cache_krope=cache_krope, state_ssm=state_ssm, g_ffn1=g_ffn1, w1_a=w1_a, w3_a=w3_a, w2_a=w2_a,
                  g_mix=g_mix, w_in=w_in, g_kv=g_kv, w_uk=w_uk, w_uv=w_uv, conv_w=conv_w, conv_b=conv_b,
                  dt_bias=dt_bias, a_log=a_log, d_skip=d_skip, g_ssm=g_ssm, w_out=w_out, g_ffn2=g_ffn2, w1_b=w1_b,
                  w3_b=w3_b, w2_b=w2_b, g_ple=g_ple, w_ple_gate=w_ple_gate, w_ple_proj=w_ple_proj)
    depth = w_in.shape[0]
    b_p, s_p, d = x_prompt.shape
    b_s, t_s, _ = x_sample.shape
    heads, rope = w_uk.shape[2], cache_krope.shape[3]
    lat = w_uk.shape[1]
    ssm_heads, headdim, d_state = state_ssm.shape[2:]
    d_inner = ssm_heads * headdim
    conv_dim = conv_w.shape[2]
    past_len = page_table.shape[1] * cache_ckv.shape[2]
    cos_p, sin_p = _rope_tables(jnp.arange(s_p), rope)
    cos_s, sin_s = (jnp.tile(t, (b_s, heads)) for t in _rope_tables(past_len + jnp.arange(t_s), rope))
    chunk_p = math.gcd(s_p, SSD_CHUNK)
    chunk_s = -(-t_s // BF16_ROWS) * BF16_ROWS
    g_fin = g_final[None]
    cache_krope_t = jnp.swapaxes(cache_krope, 2, 3)
    hp = x_prompt.reshape(b_p * s_p, d)
    hs = x_sample.reshape(b_s * t_s, d)
    outs = [[] for _ in range(8)]
    for i in range(depth):
        w = _prep_layer(i, params)
        final = i == depth - 1

        def attn_p(x, cos, sin, scale):
            q3t, k3, v3t, ckv, kr = _mix_attn_prompt(x, w["g_mix"], cos, sin, w, scale * LOG2_E)
            return _attn_prompt(q3t, k3, v3t, b_p), ckv, kr

        def attn_s(x, cos, sin, scale):
            ql, qr, ckv, kr = _mix_attn_sample(x, w["g_mix"], cos, sin, w, scale)
            new = lambda v: jnp.pad(v.reshape(b_s, t_s, -1), ((0, 0), (0, chunk_s - t_s), (0, 0))).astype(BF16)
            ol = _attn_decode(i, page_table, ql.reshape(b_s, t_s * heads, lat), qr.reshape(b_s, t_s * heads, rope),
                              new(ckv), new(kr), cache_ckv, cache_krope_t, heads)
            return _uv_proj(ol.reshape(b_s * t_s, heads * lat), w["w_uv_h"]), ckv, kr

        conv0 = jnp.zeros((b_p, CONV_PAD, conv_dim), F32)
        ssm0 = jnp.zeros((b_p, d_inner, d_state), F32)
        hp, c1, k1, cv1, s1 = _trunk_layer(hp, p_prompt[i].reshape(b_p * s_p, -1), cos_p, sin_p, conv0, ssm0, attn_p, w,
                                           g_fin, bsz=b_p, chunk=chunk_p, l_real=chunk_p, final=final)
        conv_s = jnp.pad(state_conv[i], ((0, 0), (CONV_PAD - (CONV_WIDTH - 1), 0), (0, 0)))
        hs, c2, k2, cv2, s2 = _trunk_layer(hs, p_sample[i].reshape(b_s * t_s, -1), cos_s, sin_s, conv_s,
                                           state_ssm[i].reshape(b_s, d_inner, d_state), attn_s, w, g_fin,
                                           bsz=b_s, chunk=chunk_s, l_real=t_s, final=final)
        for lst, v in zip(outs, (c1.reshape(b_p, s_p, -1), k1.reshape(b_p, s_p, -1), cv1,
                                 s1.reshape(b_p, ssm_heads, headdim, d_state),
                                 c2.reshape(b_s, t_s, -1), k2.reshape(b_s, t_s, -1), cv2,
                                 s2.reshape(b_s, ssm_heads, headdim, d_state))):
            lst.append(v)
    return (hp.reshape(b_p, s_p, d), hs.reshape(b_s, t_s, d)) + tuple(jnp.stack(o) for o in outs)
```

```python
import functools
import math

import jax
import jax.numpy as jnp
from jax import lax
from jax.experimental import pallas as pl
from jax.experimental.pallas import tpu as pltpu

F32 = jnp.float32
BF16 = jnp.bfloat16

EPS = 1e-6
HALF_STEP = 0.5
ROPE_THETA = 10000.0
SSD_CHUNK = 128
CONV_WIDTH = 4
LANES = 128
BF16_ROWS = 16
CONV_PAD = 8
V7X_VMEM_LIMIT = 56 * 1024 * 1024

FFN_ROWS = 1024
FFN_COLS = 256
MIX_ROWS = 512
ROW_TILE = 512
ATTN_Q = 1024
ATTN_K = 1024
ATTN_HEADS = 4
LOG2_E = math.log2(math.e)
DECODE_PAGES = 32
DECODE_GROUP = 8
DECODE_SLOTS = 3
SSD_SEQS = 4

_NT = (((1,), (1,)), ((), ()))
_TN = (((0,), (0,)), ((), ()))


def _params(*sem):
    return pltpu.CompilerParams(dimension_semantics=sem, vmem_limit_bytes=V7X_VMEM_LIMIT)


def _resident(v):
    return pl.BlockSpec(v.shape, lambda i: (0,) * v.ndim, pipeline_mode=pl.Buffered(1))


def _rms(x, g):
    return x * lax.rsqrt(jnp.mean(x * x, axis=-1, keepdims=True) + EPS) * g


def _dot(a, b):
    return jnp.dot(a, b, preferred_element_type=F32)


def _dot_t(a, b):
    return lax.dot_general(a, b, _NT, preferred_element_type=F32)


def _split3(v):
    hi = v.astype(BF16)
    r1 = v - hi.astype(F32)
    mid = r1.astype(BF16)
    lo = (r1 - mid.astype(F32)).astype(BF16)
    return hi, mid, lo


def _exact_dot(a, b, dims=None, split="rhs"):
    f = (lambda x, y: _dot(x, y)) if dims is None else (lambda x, y: lax.dot_general(x, y, dims, preferred_element_type=F32))
    if split == "rhs":
        h, m, l = _split3(b)
        return f(a, h) + f(a, m) + f(a, l)
    h, m, l = _split3(a)
    return f(h, b) + f(m, b) + f(l, b)


def _ffn_body(x_ref, g_ref, w1_ref, w3_ref, w2_ref, o_ref, xn_ref):
    @pl.when(pl.program_id(1) == 0)
    def _():
        x = x_ref[...]
        xn_ref[...] = _rms(x, g_ref[...]).astype(BF16)
        o_ref[...] = x

    xn = xn_ref[...]
    h1 = _dot(xn, w1_ref[...].astype(BF16))
    h3 = _dot(xn, w3_ref[...].astype(BF16))
    act = (h1 * jax.nn.sigmoid(h1) * h3).astype(BF16)
    o_ref[...] += HALF_STEP * _dot(act, w2_ref[...].astype(BF16))


def _ffn(x, g, w1, w3, w2):
    n, d = x.shape
    f = w1.shape[1]
    tm = min(FFN_ROWS, n)
    tf = FFN_COLS * (FFN_ROWS // tm)
    tf = tf if f % tf == 0 else FFN_COLS
    return pl.pallas_call(
        _ffn_body,
        out_shape=jax.ShapeDtypeStruct((n, d), F32),
        grid=(n // tm, f // tf),
        in_specs=[
            pl.BlockSpec((tm, d), lambda i, j: (i, 0)),
            pl.BlockSpec((1, d), lambda i, j: (0, 0)),
            pl.BlockSpec((d, tf), lambda i, j: (0, j)),
            pl.BlockSpec((d, tf), lambda i, j: (0, j)),
            pl.BlockSpec((tf, d), lambda i, j: (j, 0)),
        ],
        out_specs=pl.BlockSpec((tm, d), lambda i, j: (i, 0)),
        scratch_shapes=[pltpu.VMEM((tm, d), BF16)],
        compiler_params=_params("parallel", "arbitrary"),
        name="ffn_half_step",
    )(x, g, w1, w3, w2)


def _latent_and_rope_key(h, cos, sin, wckv_ref, wkr_ref, wkrr_ref, gkv_ref):
    ckv = _rms(_dot_t(h, wckv_ref[...]), gkv_ref[...])
    kr = _dot_t(h, wkr_ref[...]) * cos + _dot_t(h, wkrr_ref[...]) * sin
    return ckv, kr


def _mix_attn_prompt_body(x_ref, g_ref, cos_ref, sin_ref, cost_ref, sint_ref, wqnt_ref, wqrt_ref,
                          wckv_ref, wkr_ref, wkrr_ref, gkv_ref, wuk_ref, wuvt_ref,
                          qt_ref, k3_ref, vt_ref, ckv_ref, kr_ref, *, heads, nope, rope, scale):
    h = _rms(x_ref[...], g_ref[...]).astype(BF16)
    qn_t = _dot_t(wqnt_ref[...], h) * scale
    qr_raw = _dot_t(wqrt_ref[...], h)
    half = rope // 2
    qr_rot = jnp.concatenate([piece for g in range(heads)
                              for piece in (-qr_raw[g * rope + half:(g + 1) * rope], qr_raw[g * rope:g * rope + half])],
                             axis=0)
    qr_t = (qr_raw * cost_ref[...] + qr_rot * sint_ref[...]) * scale
    ckv, kr = _latent_and_rope_key(h, cos_ref[...], sin_ref[...], wckv_ref, wkr_ref, wkrr_ref, gkv_ref)
    ckv_b = ckv.astype(BF16)
    kn = _dot(ckv_b, wuk_ref[...])
    v_t = _dot_t(wuvt_ref[...], ckv_b)
    kr_b = kr.astype(BF16)
    vd = v_t.shape[0] // heads
    for hd in range(heads):
        qt_ref[hd, 0:nope, :] = qn_t[hd * nope:(hd + 1) * nope, :].astype(BF16)
        qt_ref[hd, nope:nope + rope, :] = qr_t[hd * rope:(hd + 1) * rope, :].astype(BF16)
        k3_ref[hd, :, 0:nope] = kn[:, hd * nope:(hd + 1) * nope].astype(BF16)
        k3_ref[hd, :, nope:nope + rope] = kr_b
        vt_ref[hd] = v_t[hd * vd:(hd + 1) * vd, :].astype(BF16)
    ckv_ref[...] = ckv
    kr_ref[...] = kr


def _mix_attn_sample_body(x_ref, g_ref, cos_ref, sin_ref, wqn_ref, wqr_ref, wqrr_ref, wckv_ref, wkr_ref, wkrr_ref,
                          gkv_ref, wukt_ref, ql_ref, qr_ref, ckv_ref, kr_ref, *, heads, nope, rope, scale):
    h = _rms(x_ref[...], g_ref[...]).astype(BF16)
    cos = cos_ref[...]
    sin = sin_ref[...]
    qn = _dot_t(h, wqn_ref[...]) * scale
    qr = (_dot_t(h, wqr_ref[...]) * cos + _dot_t(h, wqrr_ref[...]) * sin) * scale
    ckv, kr = _latent_and_rope_key(h, cos[:, :rope], sin[:, :rope], wckv_ref, wkr_ref, wkrr_ref, gkv_ref)
    lat = wukt_ref.shape[2]
    for hd in range(heads):
        qh = qn[:, hd * nope:(hd + 1) * nope].astype(BF16)
        ql_ref[:, hd * lat:(hd + 1) * lat] = _dot(qh, wukt_ref[hd]).astype(BF16)
    qr_ref[...] = qr.astype(BF16)
    ckv_ref[...] = ckv
    kr_ref[...] = kr


def _mix_attn_prompt(x, g, cos, sin, w, scale):
    n, d = x.shape
    heads, nope, rope, lat = w["heads"], w["nope"], w["rope"], w["lat"]
    tm = min(MIX_ROWS, n)
    period = cos.shape[0] // tm
    cos_t, sin_t = jnp.tile(cos, (1, heads)).T, jnp.tile(sin, (1, heads)).T
    vd = w["w_uv_t"].shape[0] // heads
    const = lambda i: (0, 0)
    row = lambda i: (i, 0)
    ws = [w["w_qn_t"], w["w_qr_t"], w["w_ckv_t"], w["w_kr_t"], w["w_kr_rot_t"], w["g_kv"], w["w_uk"], w["w_uv_t"]]
    body = functools.partial(_mix_attn_prompt_body, heads=heads, nope=nope, rope=rope, scale=scale)
    return pl.pallas_call(
        body,
        out_shape=[jax.ShapeDtypeStruct((heads, nope + rope, n), BF16),
                   jax.ShapeDtypeStruct((heads, n, nope + rope), BF16),
                   jax.ShapeDtypeStruct((heads, vd, n), BF16),
                   jax.ShapeDtypeStruct((n, lat), F32),
                   jax.ShapeDtypeStruct((n, rope), F32)],
        grid=(n // tm,),
        in_specs=[
            pl.BlockSpec((tm, d), row),
            pl.BlockSpec((1, d), const),
            pl.BlockSpec((tm, rope), lambda i: (i % period, 0)),
            pl.BlockSpec((tm, rope), lambda i: (i % period, 0)),
            pl.BlockSpec((heads * rope, tm), lambda i: (0, i % period)),
            pl.BlockSpec((heads * rope, tm), lambda i: (0, i % period)),
        ] + [_resident(v) for v in ws],
        out_specs=[pl.BlockSpec((heads, nope + rope, tm), lambda i: (0, 0, i)),
                   pl.BlockSpec((heads, tm, nope + rope), lambda i: (0, i, 0)),
                   pl.BlockSpec((heads, vd, tm), lambda i: (0, 0, i)),
                   pl.BlockSpec((tm, lat), row),
                   pl.BlockSpec((tm, rope), row)],
        compiler_params=_params("parallel"),
        name="mix_attn_prompt",
    )(x, g, cos, sin, cos_t, sin_t, *ws)


def _mix_attn_sample(x, g, cos, sin, w, scale):
    n, d = x.shape
    heads, nope, rope, lat = w["heads"], w["nope"], w["rope"], w["lat"]
    tm = min(MIX_ROWS, n)
    const = lambda i: (0, 0)
    row = lambda i: (i, 0)
    ws = [w["w_qn_t"], w["w_qr_t"], w["w_qr_rot_t"], w["w_ckv_t"], w["w_kr_t"], w["w_kr_rot_t"], w["g_kv"]]
    body = functools.partial(_mix_attn_sample_body, heads=heads, nope=nope, rope=rope, scale=scale)
    return pl.pallas_call(
        body,
        out_shape=[jax.ShapeDtypeStruct((n, heads * lat), BF16),
                   jax.ShapeDtypeStruct((n, heads * rope), BF16),
                   jax.ShapeDtypeStruct((n, lat), F32),
                   jax.ShapeDtypeStruct((n, rope), F32)],
        grid=(n // tm,),
        in_specs=[pl.BlockSpec((tm, d), row), pl.BlockSpec((1, d), const),
                  pl.BlockSpec((tm, heads * rope), row), pl.BlockSpec((tm, heads * rope), row)]
                 + [_resident(v) for v in ws] + [_resident(w["w_uk_t"])],
        out_specs=[pl.BlockSpec((tm, heads * lat), row), pl.BlockSpec((tm, heads * rope), row),
                   pl.BlockSpec((tm, lat), row), pl.BlockSpec((tm, rope), row)],
        compiler_params=_params("parallel"),
        name="mix_attn_sample",
    )(x, g, cos, sin, *ws, w["w_uk_t"])


def _mix_ssm_body(x_ref, g_ref, wz_ref, wx_ref, wdt_ref, z_ref, xbc_ref, dt_ref):
    h = _rms(x_ref[...], g_ref[...]).astype(BF16)
    z_ref[...] = _dot_t(h, wz_ref[...])
    xbc_ref[...] = _dot_t(h, wx_ref[...])
    dt_ref[...] = _dot_t(h, wdt_ref[...])


def _mix_ssm(x, g, w):
    n, d = x.shape
    tm = min(MIX_ROWS, n)
    const = lambda i: (0, 0)
    row = lambda i: (i, 0)
    ws = [w["w_z_t"], w["w_xbc_t"], w["w_dt_t"]]
    return pl.pallas_call(
        _mix_ssm_body,
        out_shape=[jax.ShapeDtypeStruct((n, wi.shape[0]), F32) for wi in ws],
        grid=(n // tm,),
        in_specs=[pl.BlockSpec((tm, d), row), pl.BlockSpec((1, d), const)] + [_resident(wi) for wi in ws],
        out_specs=[pl.BlockSpec((tm, wi.shape[0]), row) for wi in ws],
        compiler_params=_params("parallel"),
        name="mix_ssm",
    )(x, g, *ws)


def _softmax_step_t(q_t, kb, vb_t, carry, mask):
    m, l, acc_t = carry
    s_t = _dot(kb, q_t)
    if mask is not None:
        s_t = jnp.where(mask, s_t, -jnp.inf)
    m_new = jnp.maximum(m, jnp.max(s_t, axis=0, keepdims=True))
    alpha = jnp.exp2(m - m_new)
    p_t = jnp.exp2(s_t - m_new)
    l = alpha * l + jnp.sum(p_t, axis=0, keepdims=True)
    acc_t = alpha * acc_t + _dot(vb_t, p_t.astype(BF16))
    return m_new, l, acc_t


def _attn_body(qt_ref, k_ref, vt_ref, o_ref, *, tq, tk, nh):
    i = pl.program_id(2)
    vd = vt_ref.shape[1]
    qts = [qt_ref[h] for h in range(nh)]

    def blocks(off, carries, mask):
        return tuple(_softmax_step_t(qts[h], k_ref[h, pl.ds(off, tk), :], vt_ref[h, :, pl.ds(off, tk)], carries[h], mask)
                     for h in range(nh))

    def full_block(j, carries):
        return blocks(pl.multiple_of(j * tk, tk), carries, None)

    carries = tuple((jnp.full((1, tq), -jnp.inf, F32), jnp.zeros((1, tq), F32), jnp.zeros((vd, tq), F32))
                    for _ in range(nh))
    carries = lax.fori_loop(0, i * (tq // tk), full_block, carries)
    keys = lax.broadcasted_iota(jnp.int32, (tk, tq), 0)
    queries = lax.broadcasted_iota(jnp.int32, (tk, tq), 1)
    for dblk in range(tq // tk):
        carries = blocks(pl.multiple_of(i * tq + dblk * tk, tk), carries, keys + dblk * tk <= queries)
    for h, (_, l, acc_t) in enumerate(carries):
        o_ref[:, h * vd:(h + 1) * vd] = (acc_t / l).T.astype(BF16)


def _attn_prompt(q3t, k3, v3t, batch):
    heads, dq, n = q3t.shape
    vd = v3t.shape[1]
    s = n // batch
    tq = min(ATTN_Q, s)
    tk = min(ATTN_K, tq)
    nq = s // tq
    nh = math.gcd(ATTN_HEADS, heads)
    return pl.pallas_call(
        functools.partial(_attn_body, tq=tq, tk=tk, nh=nh),
        out_shape=jax.ShapeDtypeStruct((n, heads * vd), BF16),
        grid=(batch, heads // nh, nq),
        in_specs=[
            pl.BlockSpec((nh, dq, tq), lambda b, h, i: (h, 0, b * nq + i)),
            pl.BlockSpec((nh, s, dq), lambda b, h, i: (h, b, 0)),
            pl.BlockSpec((nh, vd, s), lambda b, h, i: (h, 0, b)),
        ],
        out_specs=pl.BlockSpec((tq, nh * vd), lambda b, h, i: (b * nq + i, h)),
        compiler_params=_params("parallel", "parallel", "arbitrary"),
        name="attn_prompt",
    )(q3t, k3, v3t)


def _decode_body(pt_ref, ql_ref, qr_ref, cn_ref, kn_ref, ckv_hbm, krt_hbm, o_ref,
                 ckv_buf, krt_buf, sem, m_ref, l_ref, acc_ref, *, layer, pages, group, heads):
    b, p = pl.program_id(0), pl.program_id(1)
    nb, ns = pl.num_programs(0), pl.num_programs(1)
    ring = ckv_buf.shape[0]
    total = nb * ns
    t = b * ns + p
    slot = lax.rem(t, ring)
    page = ckv_buf.shape[1] // pages

    def step_copies(ahead, k):
        tn = t + ahead
        tn = jnp.where(tn >= total, tn - total, tn)
        pid = pt_ref[lax.div(tn, ns), lax.rem(tn, ns) * pages + k]
        sl = lax.rem(t + ahead, ring)
        rows = pl.ds(k * page, page)
        return (pltpu.make_async_copy(ckv_hbm.at[layer, pid], ckv_buf.at[sl, rows, :], sem.at[sl, 0]),
                pltpu.make_async_copy(krt_hbm.at[layer, pid], krt_buf.at[sl, :, rows], sem.at[sl, 1]))

    @pl.when(t == 0)
    def _():
        for ahead in range(ring - 1):
            for k in range(pages):
                for c in step_copies(ahead, k):
                    c.start()

    for k in range(pages):
        for c in step_copies(0, k):
            c.wait()

    @pl.when(p == 0)
    def _():
        m_ref[...] = jnp.full_like(m_ref, -jnp.inf)
        l_ref[...] = jnp.zeros_like(l_ref)
        acc_ref[...] = jnp.zeros_like(acc_ref)

    last_p = p == ns - 1
    ql = ql_ref[0]
    qr = qr_ref[0]

    def update(s, kv):
        m = m_ref[...]
        m_new = jnp.maximum(m, jnp.max(s, axis=-1, keepdims=True))
        alpha = jnp.exp(m - m_new)
        pr = jnp.exp(s - m_new)
        l_ref[...] = alpha * l_ref[...] + jnp.sum(pr, axis=-1, keepdims=True)
        acc_ref[...] = alpha * acc_ref[...] + _dot(pr.astype(BF16), kv)
        m_ref[...] = m_new

    s_parts, kv_parts = [], []
    for j in range(pages // group):
        for k in range(j * group, (j + 1) * group):
            for c in step_copies(ring - 1, k):
                c.start()
        rows = pl.ds(j * group * page, group * page)
        kv_j = ckv_buf[slot, rows, :].astype(BF16)
        s_parts.append(lax.dot_general(ql, kv_j, _NT, preferred_element_type=F32)
                       + _dot(qr, krt_buf[slot, :, rows].astype(BF16)))
        kv_parts.append(kv_j)
    update(jnp.concatenate(s_parts, axis=1), jnp.concatenate(kv_parts, axis=0))

    @pl.when(last_p)
    def _():
        rows, nk = ql.shape[0], cn_ref.shape[1]
        qtok = lax.broadcasted_iota(jnp.int32, (rows, nk), 0) // heads
        ktok = lax.broadcasted_iota(jnp.int32, (rows, nk), 1)
        kvn = cn_ref[0]
        s = (lax.dot_general(ql, kvn, _NT, preferred_element_type=F32)
             + lax.dot_general(qr, kn_ref[0], _NT, preferred_element_type=F32))
        update(jnp.where(ktok <= qtok, s, -jnp.inf), kvn)
        o_ref[0] = (acc_ref[...] / l_ref[...]).astype(BF16)

    @pl.when(t == total - 1)
    def _():
        for ahead in range(1, ring):
            for k in range(pages):
                for c in step_copies(ahead, k):
                    c.wait()


def _attn_decode(layer, page_table, ql, qr, ckv_new, kr_new, cache_ckv, cache_krope_t, heads):
    bsz, rows, lat = ql.shape
    rope = qr.shape[2]
    n_pages = page_table.shape[1]
    page = cache_ckv.shape[2]
    g = math.gcd(DECODE_PAGES, n_pages)
    nk = ckv_new.shape[1]
    bmap = lambda b, p, pt: (b, 0, 0)
    grid_spec = pltpu.PrefetchScalarGridSpec(
        num_scalar_prefetch=1,
        grid=(bsz, n_pages // g),
        in_specs=[
            pl.BlockSpec((1, rows, lat), bmap),
            pl.BlockSpec((1, rows, rope), bmap),
            pl.BlockSpec((1, nk, lat), bmap),
            pl.BlockSpec((1, nk, rope), bmap),
            pl.BlockSpec(memory_space=pl.ANY),
            pl.BlockSpec(memory_space=pl.ANY),
        ],
        out_specs=pl.BlockSpec((1, rows, lat), bmap),
        scratch_shapes=[pltpu.VMEM((DECODE_SLOTS, g * page, lat), F32), pltpu.VMEM((DECODE_SLOTS, rope, g * page), F32),
                        pltpu.SemaphoreType.DMA((DECODE_SLOTS, 2)),
                        pltpu.VMEM((rows, 1), F32), pltpu.VMEM((rows, 1), F32), pltpu.VMEM((rows, lat), F32)],
    )
    return pl.pallas_call(
        functools.partial(_decode_body, layer=layer, pages=g, group=math.gcd(DECODE_GROUP, g), heads=heads),
        grid_spec=grid_spec,
        out_shape=jax.ShapeDtypeStruct((bsz, rows, lat), BF16),
        compiler_params=_params("arbitrary", "arbitrary"),
        name="attn_decode",
    )(page_table, ql, qr, ckv_new, kr_new, cache_ckv, cache_krope_t)


def _uv_body(ol_ref, wuv_ref, o_ref, *, heads):
    lat, vd = wuv_ref.shape[1], wuv_ref.shape[2]
    for hd in range(heads):
        o_ref[:, hd * vd:(hd + 1) * vd] = _dot(ol_ref[:, hd * lat:(hd + 1) * lat], wuv_ref[hd]).astype(BF16)


def _uv_proj(ol, wuv_h):
    n = ol.shape[0]
    heads, lat, vd = wuv_h.shape
    tm = min(ROW_TILE, n)
    return pl.pallas_call(
        functools.partial(_uv_body, heads=heads),
        out_shape=jax.ShapeDtypeStruct((n, heads * vd), BF16),
        grid=(n // tm,),
        in_specs=[pl.BlockSpec((tm, heads * lat), lambda i: (i, 0)), pl.BlockSpec(wuv_h.shape, lambda i: (0, 0, 0))],
        out_specs=pl.BlockSpec((tm, heads * vd), lambda i: (i, 0)),
        compiler_params=_params("parallel"),
        name="uv_proj",
    )(ol, wuv_h)


def _ssd_body(xbc_ref, z_ref, dt_ref, cprev_ref, sprev_ref, cw_ref, cb_ref, dtb_ref, alog_ref, dsk_ref, g_ref, r_ref,
              y_ref, cnew_ref, snew_ref, xp_ref, *, nseq, **dims):
    for sq in range(nseq):
        _ssd_seq(sq, xbc_ref, z_ref, dt_ref, cprev_ref, sprev_ref, cw_ref, cb_ref, dtb_ref, alog_ref, dsk_ref, g_ref,
                 r_ref, y_ref, cnew_ref, snew_ref, xp_ref, **dims)


def _ssd_seq(sq, xbc_ref, z_ref, dt_ref, cprev_ref, sprev_ref, cw_ref, cb_ref, dtb_ref, alog_ref, dsk_ref, g_ref, r_ref,
             y_ref, cnew_ref, snew_ref, xp_ref, *, chunk, l_real, heads, headdim, groups, d_state, single_chunk):
    c = pl.program_id(1)
    last = pl.num_programs(1) - 1
    d_inner = heads * headdim
    gw = d_inner // groups
    hpg = heads // groups
    state_in = sprev_ref if single_chunk else snew_ref

    @pl.when(c == 0)
    def _():
        if not single_chunk:
            snew_ref[sq] = sprev_ref[sq]
        xp_ref[sq, 0:CONV_PAD, :] = cprev_ref[sq]

    xp_ref[sq, CONV_PAD:CONV_PAD + chunk, :] = xbc_ref[sq]
    cw = cw_ref[...]
    y = cb_ref[...]
    for k in range(CONV_WIDTH):
        y = y + xp_ref[sq, CONV_PAD - (CONV_WIDTH - 1) + k:CONV_PAD - (CONV_WIDTH - 1) + k + chunk, :] * cw[k:k + 1, :]
    xbc = y * jax.nn.sigmoid(y)
    tail = xp_ref[sq, l_real:l_real + CONV_PAD, :]
    xp_ref[sq, 0:CONV_PAD, :] = tail

    @pl.when(c == last)
    def _():
        cnew_ref[sq] = tail

    xs = xbc[:, :d_inner]
    bm = xbc[:, d_inner:d_inner + groups * d_state].astype(BF16)
    cm = xbc[:, d_inner + groups * d_state:].astype(BF16)

    t_row = lax.broadcasted_iota(jnp.int32, (chunk, LANES), 0)
    dtr = dt_ref[sq] + dtb_ref[...]
    dt = jnp.maximum(dtr, 0.0) + jnp.log1p(jnp.exp(-jnp.abs(dtr)))
    if l_real < chunk:
        dt = jnp.where(t_row < l_real, dt, 0.0)
    a = dt * (-jnp.exp(alog_ref[...]))
    tri = (lax.broadcasted_iota(jnp.int32, (chunk, chunk), 0) >= lax.broadcasted_iota(jnp.int32, (chunk, chunk), 1))
    cs = _exact_dot(jnp.where(tri, 1.0, 0.0).astype(BF16), a)
    cs_last = cs[chunk - 1:chunk, :]
    ecs = jnp.exp(cs)
    dec = jnp.exp(cs_last - cs)
    expanded = _exact_dot(jnp.concatenate([dt, ecs, dec], axis=0), r_ref[...], split="lhs")
    dt_e = expanded[0:chunk]
    ecs_e = expanded[chunk:2 * chunk]
    dec_e = expanded[2 * chunk:3 * chunk]
    xdt = xs * dt_e

    eye = (lax.broadcasted_iota(jnp.int32, (LANES, LANES), 0) == lax.broadcasted_iota(jnp.int32, (LANES, LANES), 1))
    cs_t = _exact_dot(jnp.where(eye, 1.0, 0.0).astype(BF16), cs, dims=_NT)
    xdt_p = xdt.astype(BF16)
    bm_p = bm
    causal = lax.broadcasted_iota(jnp.int32, (chunk, chunk), 1) <= lax.broadcasted_iota(jnp.int32, (chunk, chunk), 0)
    ecl = jnp.exp(cs_last)

    y_groups = []
    for g in range(groups):
        cg = cm[:, g * d_state:(g + 1) * d_state]
        cb_mat = lax.dot_general(cg, bm_p[:, g * d_state:(g + 1) * d_state], _NT, preferred_element_type=F32)
        sg = state_in[sq, g * gw:(g + 1) * gw, :]
        y_off = lax.dot_general(cg, sg.astype(BF16), _NT, preferred_element_type=F32)
        y_heads = []
        for hh in range(hpg):
            hd = g * hpg + hh
            seg = jnp.exp(jnp.where(causal, cs[:, hd:hd + 1] - cs_t[hd:hd + 1, :], -jnp.inf))
            y_heads.append(_dot((cb_mat * seg).astype(BF16), xdt_p[:, hd * headdim:(hd + 1) * headdim]))
        y_groups.append(jnp.concatenate(y_heads, axis=1) + y_off * ecs_e[:, g * gw:(g + 1) * gw])
        xd = (xdt[:, g * gw:(g + 1) * gw] * dec_e[:, g * gw:(g + 1) * gw]).astype(BF16)
        upd = lax.dot_general(xd, bm[:, g * d_state:(g + 1) * d_state], _TN, preferred_element_type=F32)
        for hh in range(hpg):
            hd = g * hpg + hh
            r0 = g * gw + hh * headdim
            snew_ref[sq, r0:r0 + headdim, :] = (sg[hh * headdim:(hh + 1) * headdim, :] * ecl[0:1, hd:hd + 1]
                                                + upd[hh * headdim:(hh + 1) * headdim, :])

    yv = jnp.concatenate(y_groups, axis=1) + xs * dsk_ref[...]
    zv = z_ref[sq]
    yv = yv * (zv * jax.nn.sigmoid(zv))
    gv = g_ref[...]
    outs = [_rms(yv[:, g * gw:(g + 1) * gw], gv[:, g * gw:(g + 1) * gw]) for g in range(groups)]
    y_ref[sq] = jnp.concatenate(outs, axis=1).astype(BF16)


def _ssd(xbc, z, dt, conv_prev, state_prev, w, *, chunk, l_real):
    bsz, t, cd = xbc.shape
    heads, headdim, groups, d_state = w["ssm_heads"], w["ssm_headdim"], w["ssm_groups"], w["ssm_state"]
    d_inner = heads * headdim
    nc = t // chunk
    nseq = math.gcd(SSD_SEQS, bsz)
    tok = lambda b, c: (b, c, 0)
    seq = lambda b, c: (b, 0, 0)
    const = lambda b, c: (0, 0)
    consts = [w["conv_w"], w["conv_b"], w["dt_bias"], w["a_log"], w["d_skip"], w["g_ssm"], w["head_expand"]]
    body = functools.partial(_ssd_body, nseq=nseq, chunk=chunk, l_real=l_real, heads=heads, headdim=headdim,
                             groups=groups, d_state=d_state, single_chunk=nc == 1)
    return pl.pallas_call(
        body,
        out_shape=[jax.ShapeDtypeStruct((bsz, t, d_inner), BF16),
                   jax.ShapeDtypeStruct((bsz, CONV_PAD, cd), F32),
                   jax.ShapeDtypeStruct((bsz, d_inner, d_state), F32)],
        grid=(bsz // nseq, nc),
        in_specs=[
            pl.BlockSpec((nseq, chunk, cd), tok),
            pl.BlockSpec((nseq, chunk, d_inner), tok),
            pl.BlockSpec((nseq, chunk, LANES), tok),
            pl.BlockSpec((nseq, CONV_PAD, cd), seq),
            pl.BlockSpec((nseq, d_inner, d_state), seq),
        ] + [pl.BlockSpec(v.shape, const) for v in consts],
        out_specs=[pl.BlockSpec((nseq, chunk, d_inner), tok),
                   pl.BlockSpec((nseq, CONV_PAD, cd), seq),
                   pl.BlockSpec((nseq, d_inner, d_state), seq)],
        scratch_shapes=[pltpu.VMEM((nseq, CONV_PAD + chunk, cd), F32)],
        compiler_params=_params("parallel", "arbitrary"),
        name="ssd_mixer",
    )(xbc, z, dt, conv_prev, state_prev, *consts)


def _out_proj_body(x_ref, oa_ref, os_ref, wa_ref, ws_ref, o_ref):
    o_ref[...] = x_ref[...] + _dot(oa_ref[...], wa_ref[...]) + _dot(os_ref[...], ws_ref[...])


def _out_proj(x, oa, osm, wa, ws):
    n, d = x.shape
    tm = min(ROW_TILE, n)
    row = lambda i: (i, 0)
    const = lambda i: (0, 0)
    return pl.pallas_call(
        _out_proj_body,
        out_shape=jax.ShapeDtypeStruct((n, d), F32),
        grid=(n // tm,),
        in_specs=[pl.BlockSpec((tm, d), row), pl.BlockSpec((tm, oa.shape[1]), row), pl.BlockSpec((tm, osm.shape[1]), row),
                  pl.BlockSpec(wa.shape, const), pl.BlockSpec(ws.shape, const)],
        out_specs=pl.BlockSpec((tm, d), row),
        compiler_params=_params("parallel"),
        name="out_proj",
    )(x, oa, osm, wa, ws)


def _ple_body(x_ref, pe_ref, g_ref, wg_ref, wp_ref, gf_ref, o_ref, *, final):
    x = x_ref[...]
    gate = jax.nn.sigmoid(_dot(_rms(x, g_ref[...]).astype(BF16), wg_ref[...]))
    x = x + gate * _dot(pe_ref[...].astype(BF16), wp_ref[...])
    o_ref[...] = _rms(x, gf_ref[...]) if final else x


def _ple(x, pe, g, wg, wp, g_final, final):
    n, d = x.shape
    tm = min(ROW_TILE, n)
    row = lambda i: (i, 0)
    const = lambda i: (0, 0)
    return pl.pallas_call(
        functools.partial(_ple_body, final=final),
        out_shape=jax.ShapeDtypeStruct((n, d), F32),
        grid=(n // tm,),
        in_specs=[pl.BlockSpec((tm, d), row), pl.BlockSpec((tm, pe.shape[1]), row), pl.BlockSpec((1, d), const),
                  pl.BlockSpec(wg.shape, const), pl.BlockSpec(wp.shape, const), pl.BlockSpec((1, d), const)],
        out_specs=pl.BlockSpec((tm, d), row),
        compiler_params=_params("parallel"),
        name="ple_final",
    )(x, pe, g, wg, wp, g_final)


def _rope_tables(pos, rope):
    half = rope // 2
    inv = ROPE_THETA ** (-jnp.arange(half, dtype=F32) / half)
    ang = pos.astype(F32)[:, None] * inv[None, :]
    cos, sin = jnp.cos(ang), jnp.sin(ang)
    return jnp.concatenate([cos, cos], axis=-1), jnp.concatenate([sin, sin], axis=-1)


def _rot_rows(w_t, rope):
    d = w_t.shape[1]
    wr = w_t.reshape(-1, 2, rope // 2, d)
    return jnp.stack([-wr[:, 1], wr[:, 0]], axis=1).reshape(-1, d)


def _prep_layer(i, p):
    d = p["w_in"].shape[1]
    lat, heads, nope = p["w_uk"].shape[1:]
    vd = p["w_uv"].shape[3]
    rope = p["cache_krope"].shape[3]
    ssm_heads, headdim, d_state = p["state_ssm"].shape[2:]
    d_inner = ssm_heads * headdim
    conv_dim = p["conv_w"].shape[2]
    groups = (conv_dim - d_inner) // (2 * d_state)
    q_dim = heads * (nope + rope)
    w_in_t = jnp.swapaxes(p["w_in"], 1, 2)[i]
    o = 0
    wq_t = w_in_t[o:o + q_dim].reshape(heads, nope + rope, d); o += q_dim
    w_ckv_t = w_in_t[o:o + lat]; o += lat
    w_kr_t = w_in_t[o:o + rope]; o += rope
    w_z_t = w_in_t[o:o + d_inner]; o += d_inner
    w_xbc_t = w_in_t[o:o + conv_dim]; o += conv_dim
    w_dt_t = w_in_t[o:o + ssm_heads]
    w_qn_t = wq_t[:, :nope].reshape(heads * nope, d)
    w_qr_t = wq_t[:, nope:].reshape(heads * rope, d)
    b = lambda t: t.astype(BF16)
    pad_heads = lambda v: jnp.pad(v.reshape(1, -1), ((0, 0), (0, LANES - ssm_heads)))
    head_of_channel = jnp.arange(d_inner) // headdim
    w_out = p["w_out"][i]
    return dict(
        heads=heads, nope=nope, rope=rope, lat=lat,
        ssm_heads=ssm_heads, ssm_headdim=headdim, ssm_groups=groups, ssm_state=d_state,
        g_ffn1=p["g_ffn1"][i][None], w1_a=p["w1_a"][i], w3_a=p["w3_a"][i], w2_a=p["w2_a"][i],
        g_ffn2=p["g_ffn2"][i][None], w1_b=p["w1_b"][i], w3_b=p["w3_b"][i], w2_b=p["w2_b"][i],
        g_mix=p["g_mix"][i][None],
        w_qn_t=b(w_qn_t), w_qr_t=b(w_qr_t), w_qr_rot_t=b(_rot_rows(w_qr_t, rope)),
        w_ckv_t=b(w_ckv_t), w_kr_t=b(w_kr_t), w_kr_rot_t=b(_rot_rows(w_kr_t, rope)), g_kv=p["g_kv"][i][None],
        w_uk=b(p["w_uk"][i].reshape(lat, heads * nope)), w_uv_t=b(p["w_uv"][i].reshape(lat, heads * vd).T),
        w_uk_t=b(jnp.transpose(p["w_uk"][i], (1, 2, 0))), w_uv_h=b(jnp.transpose(p["w_uv"][i], (1, 0, 2))),
        w_z_t=b(w_z_t), w_xbc_t=b(w_xbc_t), w_dt_t=b(jnp.pad(w_dt_t, ((0, LANES - ssm_heads), (0, 0)))),
        conv_w=p["conv_w"][i], conv_b=p["conv_b"][i][None], dt_bias=pad_heads(p["dt_bias"][i]),
        a_log=pad_heads(p["a_log"][i]), d_skip=jnp.repeat(p["d_skip"][i], headdim)[None], g_ssm=p["g_ssm"][i][None],
        head_expand=(jnp.arange(LANES)[:, None] == head_of_channel[None, :]).astype(BF16),
        w_out_a=b(w_out[:heads * vd]), w_out_s=b(w_out[heads * vd:]),
        g_ple=p["g_ple"][i][None], w_ple_gate=b(p["w_ple_gate"][i]), w_ple_proj=b(p["w_ple_proj"][i]),
    )


def _trunk_layer(x, pe, cos, sin, conv_prev, state_prev, attn, w, g_final, *, bsz, chunk, l_real, final):
    n, d = x.shape
    t = n // bsz
    scale = (w["nope"] + w["rope"]) ** -0.5
    x = _ffn(x, w["g_ffn1"], w["w1_a"], w["w3_a"], w["w2_a"])
    o_att, ckv, kr = attn(x, cos, sin, scale)
    z, xbc, dt = _mix_ssm(x, w["g_mix"], w)
    seq = lambda v: v.reshape(bsz, t, v.shape[1])
    if t % chunk:
        pad = lambda v: jnp.pad(seq(v), ((0, 0), (0, chunk - t), (0, 0)))
    else:
        pad = seq
    o_ssm, conv_new, ssm_new = _ssd(pad(xbc), pad(z), pad(dt), conv_prev, state_prev, w, chunk=chunk, l_real=l_real)
    o_ssm = o_ssm[:, :t].reshape(n, -1)
    x = _out_proj(x, o_att, o_ssm, w["w_out_a"], w["w_out_s"])
    x = _ffn(x, w["g_ffn2"], w["w1_b"], w["w3_b"], w["w2_b"])
    x = _ple(x, pe, w["g_ple"], w["w_ple_gate"], w["w_ple_proj"], g_final, final)
    return x, ckv, kr, conv_new[:, CONV_PAD - (CONV_WIDTH - 1):], ssm_new


def kernel(x_prompt, x_sample, p_prompt, p_sample, cache_ckv, cache_krope, state_conv, state_ssm, page_table, g_ffn1, w1_a, w3_a, w2_a, g_mix, w_in, g_kv, w_uk, w_uv, conv_w, conv_b, dt_bias, a_log, d_skip, g_ssm, w_out, g_ffn2, w1_b, w3_b, w2_b, g_ple, w_ple_gate, w_ple_proj, g_final):
    params = dict(cache_krope=cache_krope, state_ssm=state_ssm, g_ffn1=g_ffn1, w1_a=w1_a, w3_a=w3_a, w2_a=w2_a,
                  g_mix=g_mix, w_in=w_in, g_kv=g_kv, w_uk=w_uk, w_uv=w_uv, conv_w=conv_w, conv_b=conv_b,
                  dt_bias=dt_bias, a_log=a_log, d_skip=d_skip, g_ssm=g_ssm, w_out=w_out, g_ffn2=g_ffn2, w1_b=w1_b,
                  w3_b=w3_b, w2_b=w2_b, g_ple=g_ple, w_ple_gate=w_ple_gate, w_ple_proj=w_ple_proj)
    depth = w_in.shape[0]
    b_p, s_p, d = x_prompt.shape
    b_s, t_s, _ = x_sample.shape
    heads, rope = w_uk.shape[2], cache_krope.shape[3]
    lat = w_uk.shape[1]
    ssm_heads, headdim, d_state = state_ssm.shape[2:]
    d_inner = ssm_heads * headdim
    conv_dim = conv_w.shape[2]
    past_len = page_table.shape[1] * cache_ckv.shape[2]
    cos_p, sin_p = _rope_tables(jnp.arange(s_p), rope)
    cos_s, sin_s = (jnp.tile(t, (b_s, heads)) for t in _rope_tables(past_len + jnp.arange(t_s), rope))
    chunk_p = math.gcd(s_p, SSD_CHUNK)
    chunk_s = -(-t_s // BF16_ROWS) * BF16_ROWS
    g_fin = g_final[None]
    cache_krope_t = jnp.swapaxes(cache_krope, 2, 3)
    hp = x_prompt.reshape(b_p * s_p, d)
    hs = x_sample.reshape(b_s * t_s, d)
    outs = [[] for _ in range(8)]
    for i in range(depth):
        w = _prep_layer(i, params)
        final = i == depth - 1

        def attn_p(x, cos, sin, scale):
            q3t, k3, v3t, ckv, kr = _mix_attn_prompt(x, w["g_mix"], cos, sin, w, scale * LOG2_E)
            return _attn_prompt(q3t, k3, v3t, b_p), ckv, kr

        def attn_s(x, cos, sin, scale):
            ql, qr, ckv, kr = _mix_attn_sample(x, w["g_mix"], cos, sin, w, scale)
            new = lambda v: jnp.pad(v.reshape(b_s, t_s, -1), ((0, 0), (0, chunk_s - t_s), (0, 0))).astype(BF16)
            ol = _attn_decode(i, page_table, ql.reshape(b_s, t_s * heads, lat), qr.reshape(b_s, t_s * heads, rope),
                              new(ckv), new(kr), cache_ckv, cache_krope_t, heads)
            return _uv_proj(ol.reshape(b_s * t_s, heads * lat), w["w_uv_h"]), ckv, kr

        conv0 = jnp.zeros((b_p, CONV_PAD, conv_dim), F32)
        ssm0 = jnp.zeros((b_p, d_inner, d_state), F32)
        hp, c1, k1, cv1, s1 = _trunk_layer(hp, p_prompt[i].reshape(b_p * s_p, -1), cos_p, sin_p, conv0, ssm0, attn_p, w,
                                           g_fin, bsz=b_p, chunk=chunk_p, l_real=chunk_p, final=final)
        conv_s = jnp.pad(state_conv[i], ((0, 0), (CONV_PAD - (CONV_WIDTH - 1), 0), (0, 0)))
        hs, c2, k2, cv2, s2 = _trunk_layer(hs, p_sample[i].reshape(b_s * t_s, -1), cos_s, sin_s, conv_s,
                                           state_ssm[i].reshape(b_s, d_inner, d_state), attn_s, w, g_fin,
                                           bsz=b_s, chunk=chunk_s, l_real=t_s, final=final)
        for lst, v in zip(outs, (c1.reshape(b_p, s_p, -1), k1.reshape(b_p, s_p, -1), cv1,
                                 s1.reshape(b_p, ssm_heads, headdim, d_state),
                                 c2.reshape(b_s, t_s, -1), k2.reshape(b_s, t_s, -1), cv2,
                                 s2.reshape(b_s, ssm_heads, headdim, d_state))):
            lst.append(v)
    return (hp.reshape(b_p, s_p, d), hs.reshape(b_s, t_s, d)) + tuple(jnp.stack(o) for o in outs)
```

```python
import functools
import math

import jax
import jax.numpy as jnp
from jax import lax
from jax.experimental import pallas as pl
from jax.experimental.pallas import tpu as pltpu

F32 = jnp.float32
BF16 = jnp.bfloat16

EPS = 1e-6
HALF_STEP = 0.5
ROPE_THETA = 10000.0
SSD_CHUNK = 128
CONV_WIDTH = 4
LANES = 128
BF16_ROWS = 16
CONV_PAD = 8
V7X_VMEM_LIMIT = 56 * 1024 * 1024

FFN_ROWS = 1024
FFN_COLS = 256
MIX_ROWS = 512
ROW_TILE = 512
ATTN_Q = 1024
ATTN_K = 1024
ATTN_HEADS = 4
LOG2_E = math.log2(math.e)
DECODE_PAGES = 32
DECODE_GROUP = 8
DECODE_SLOTS = 3
DECODE_CHAINS = 2
SSD_SEQS = 4

_NT = (((1,), (1,)), ((), ()))
_TN = (((0,), (0,)), ((), ()))


def _params(*sem):
    return pltpu.CompilerParams(dimension_semantics=sem, vmem_limit_bytes=V7X_VMEM_LIMIT)


def _resident(v):
    return pl.BlockSpec(v.shape, lambda i: (0,) * v.ndim, pipeline_mode=pl.Buffered(1))


def _rms(x, g):
    return x * lax.rsqrt(jnp.mean(x * x, axis=-1, keepdims=True) + EPS) * g


def _dot(a, b):
    return jnp.dot(a, b, preferred_element_type=F32)


def _dot_t(a, b):
    return lax.dot_general(a, b, _NT, preferred_element_type=F32)


def _split3(v):
    hi = v.astype(BF16)
    r1 = v - hi.astype(F32)
    mid = r1.astype(BF16)
    lo = (r1 - mid.astype(F32)).astype(BF16)
    return hi, mid, lo


def _exact_dot(a, b, dims=None, split="rhs"):
    f = (lambda x, y: _dot(x, y)) if dims is None else (lambda x, y: lax.dot_general(x, y, dims, preferred_element_type=F32))
    if split == "rhs":
        h, m, l = _split3(b)
        return f(a, h) + f(a, m) + f(a, l)
    h, m, l = _split3(a)
    return f(h, b) + f(m, b) + f(l, b)


def _ffn_body(x_ref, g_ref, w1_ref, w3_ref, w2_ref, o_ref, xn_ref):
    @pl.when(pl.program_id(1) == 0)
    def _():
        x = x_ref[...]
        xn_ref[...] = _rms(x, g_ref[...]).astype(BF16)
        o_ref[...] = x

    xn = xn_ref[...]
    h1 = _dot(xn, w1_ref[...].astype(BF16))
    h3 = _dot(xn, w3_ref[...].astype(BF16))
    act = (h1 * jax.nn.sigmoid(h1) * h3).astype(BF16)
    o_ref[...] += HALF_STEP * _dot(act, w2_ref[...].astype(BF16))


def _ffn(x, g, w1, w3, w2):
    n, d = x.shape
    f = w1.shape[1]
    tm = min(FFN_ROWS, n)
    tf = FFN_COLS * (FFN_ROWS // tm)
    tf = tf if f % tf == 0 else FFN_COLS
    return pl.pallas_call(
        _ffn_body,
        out_shape=jax.ShapeDtypeStruct((n, d), F32),
        grid=(n // tm, f // tf),
        in_specs=[
            pl.BlockSpec((tm, d), lambda i, j: (i, 0)),
            pl.BlockSpec((1, d), lambda i, j: (0, 0)),
            pl.BlockSpec((d, tf), lambda i, j: (0, j)),
            pl.BlockSpec((d, tf), lambda i, j: (0, j)),
            pl.BlockSpec((tf, d), lambda i, j: (j, 0)),
        ],
        out_specs=pl.BlockSpec((tm, d), lambda i, j: (i, 0)),
        scratch_shapes=[pltpu.VMEM((tm, d), BF16)],
        compiler_params=_params("parallel", "arbitrary"),
        name="ffn_half_step",
    )(x, g, w1, w3, w2)


def _latent_and_rope_key(h, cos, sin, wckv_ref, wkr_ref, wkrr_ref, gkv_ref):
    ckv = _rms(_dot_t(h, wckv_ref[...]), gkv_ref[...])
    kr = _dot_t(h, wkr_ref[...]) * cos + _dot_t(h, wkrr_ref[...]) * sin
    return ckv, kr


def _mix_attn_prompt_body(x_ref, g_ref, cos_ref, sin_ref, cost_ref, sint_ref, wqnt_ref, wqrt_ref,
                          wckv_ref, wkr_ref, wkrr_ref, gkv_ref, wuk_ref, wuvt_ref,
                          qt_ref, k3_ref, vt_ref, ckv_ref, kr_ref, *, heads, nope, rope, scale):
    h = _rms(x_ref[...], g_ref[...]).astype(BF16)
    qn_t = _dot_t(wqnt_ref[...], h) * scale
    qr_raw = _dot_t(wqrt_ref[...], h)
    half = rope // 2
    qr_rot = jnp.concatenate([piece for g in range(heads)
                              for piece in (-qr_raw[g * rope + half:(g + 1) * rope], qr_raw[g * rope:g * rope + half])],
                             axis=0)
    qr_t = (qr_raw * cost_ref[...] + qr_rot * sint_ref[...]) * scale
    ckv, kr = _latent_and_rope_key(h, cos_ref[...], sin_ref[...], wckv_ref, wkr_ref, wkrr_ref, gkv_ref)
    ckv_b = ckv.astype(BF16)
    kn = _dot(ckv_b, wuk_ref[...])
    v_t = _dot_t(wuvt_ref[...], ckv_b)
    kr_b = kr.astype(BF16)
    vd = v_t.shape[0] // heads
    for hd in range(heads):
        qt_ref[hd, 0:nope, :] = qn_t[hd * nope:(hd + 1) * nope, :].astype(BF16)
        qt_ref[hd, nope:nope + rope, :] = qr_t[hd * rope:(hd + 1) * rope, :].astype(BF16)
        k3_ref[hd, :, 0:nope] = kn[:, hd * nope:(hd + 1) * nope].astype(BF16)
        k3_ref[hd, :, nope:nope + rope] = kr_b
        vt_ref[hd] = v_t[hd * vd:(hd + 1) * vd, :].astype(BF16)
    ckv_ref[...] = ckv
    kr_ref[...] = kr


def _mix_attn_sample_body(x_ref, g_ref, cos_ref, sin_ref, wqn_ref, wqr_ref, wqrr_ref, wckv_ref, wkr_ref, wkrr_ref,
                          gkv_ref, wukt_ref, ql_ref, qr_ref, ckv_ref, kr_ref, *, heads, nope, rope, scale):
    h = _rms(x_ref[...], g_ref[...]).astype(BF16)
    cos = cos_ref[...]
    sin = sin_ref[...]
    qn = _dot_t(h, wqn_ref[...]) * scale
    qr = (_dot_t(h, wqr_ref[...]) * cos + _dot_t(h, wqrr_ref[...]) * sin) * scale
    ckv, kr = _latent_and_rope_key(h, cos[:, :rope], sin[:, :rope], wckv_ref, wkr_ref, wkrr_ref, gkv_ref)
    lat = wukt_ref.shape[2]
    for hd in range(heads):
        qh = qn[:, hd * nope:(hd + 1) * nope].astype(BF16)
        ql_ref[:, hd * lat:(hd + 1) * lat] = _dot(qh, wukt_ref[hd]).astype(BF16)
    qr_ref[...] = qr.astype(BF16)
    ckv_ref[...] = ckv
    kr_ref[...] = kr


def _mix_attn_prompt(x, g, cos, sin, w, scale):
    n, d = x.shape
    heads, nope, rope, lat = w["heads"], w["nope"], w["rope"], w["lat"]
    tm = min(MIX_ROWS, n)
    period = cos.shape[0] // tm
    cos_t, sin_t = jnp.tile(cos, (1, heads)).T, jnp.tile(sin, (1, heads)).T
    vd = w["w_uv_t"].shape[0] // heads
    const = lambda i: (0, 0)
    row = lambda i: (i, 0)
    ws = [w["w_qn_t"], w["w_qr_t"], w["w_ckv_t"], w["w_kr_t"], w["w_kr_rot_t"], w["g_kv"], w["w_uk"], w["w_uv_t"]]
    body = functools.partial(_mix_attn_prompt_body, heads=heads, nope=nope, rope=rope, scale=scale)
    return pl.pallas_call(
        body,
        out_shape=[jax.ShapeDtypeStruct((heads, nope + rope, n), BF16),
                   jax.ShapeDtypeStruct((heads, n, nope + rope), BF16),
                   jax.ShapeDtypeStruct((heads, vd, n), BF16),
                   jax.ShapeDtypeStruct((n, lat), F32),
                   jax.ShapeDtypeStruct((n, rope), F32)],
        grid=(n // tm,),
        in_specs=[
            pl.BlockSpec((tm, d), row),
            pl.BlockSpec((1, d), const),
            pl.BlockSpec((tm, rope), lambda i: (i % period, 0)),
            pl.BlockSpec((tm, rope), lambda i: (i % period, 0)),
            pl.BlockSpec((heads * rope, tm), lambda i: (0, i % period)),
            pl.BlockSpec((heads * rope, tm), lambda i: (0, i % period)),
        ] + [_resident(v) for v in ws],
        out_specs=[pl.BlockSpec((heads, nope + rope, tm), lambda i: (0, 0, i)),
                   pl.BlockSpec((heads, tm, nope + rope), lambda i: (0, i, 0)),
                   pl.BlockSpec((heads, vd, tm), lambda i: (0, 0, i)),
                   pl.BlockSpec((tm, lat), row),
                   pl.BlockSpec((tm, rope), row)],
        compiler_params=_params("parallel"),
        name="mix_attn_prompt",
    )(x, g, cos, sin, cos_t, sin_t, *ws)


def _mix_attn_sample(x, g, cos, sin, w, scale):
    n, d = x.shape
    heads, nope, rope, lat = w["heads"], w["nope"], w["rope"], w["lat"]
    tm = min(MIX_ROWS, n)
    const = lambda i: (0, 0)
    row = lambda i: (i, 0)
    ws = [w["w_qn_t"], w["w_qr_t"], w["w_qr_rot_t"], w["w_ckv_t"], w["w_kr_t"], w["w_kr_rot_t"], w["g_kv"]]
    body = functools.partial(_mix_attn_sample_body, heads=heads, nope=nope, rope=rope, scale=scale)
    return pl.pallas_call(
        body,
        out_shape=[jax.ShapeDtypeStruct((n, heads * lat), BF16),
                   jax.ShapeDtypeStruct((n, heads * rope), BF16),
                   jax.ShapeDtypeStruct((n, lat), F32),
                   jax.ShapeDtypeStruct((n, rope), F32)],
        grid=(n // tm,),
        in_specs=[pl.BlockSpec((tm, d), row), pl.BlockSpec((1, d), const),
                  pl.BlockSpec((tm, heads * rope), row), pl.BlockSpec((tm, heads * rope), row)]
                 + [_resident(v) for v in ws] + [_resident(w["w_uk_t"])],
        out_specs=[pl.BlockSpec((tm, heads * lat), row), pl.BlockSpec((tm, heads * rope), row),
                   pl.BlockSpec((tm, lat), row), pl.BlockSpec((tm, rope), row)],
        compiler_params=_params("parallel"),
        name="mix_attn_sample",
    )(x, g, cos, sin, *ws, w["w_uk_t"])


def _mix_ssm_body(x_ref, g_ref, wz_ref, wx_ref, wdt_ref, z_ref, xbc_ref, dt_ref):
    h = _rms(x_ref[...], g_ref[...]).astype(BF16)
    z_ref[...] = _dot_t(h, wz_ref[...])
    xbc_ref[...] = _dot_t(h, wx_ref[...])
    dt_ref[...] = _dot_t(h, wdt_ref[...])


def _mix_ssm(x, g, w):
    n, d = x.shape
    tm = min(MIX_ROWS, n)
    const = lambda i: (0, 0)
    row = lambda i: (i, 0)
    ws = [w["w_z_t"], w["w_xbc_t"], w["w_dt_t"]]
    return pl.pallas_call(
        _mix_ssm_body,
        out_shape=[jax.ShapeDtypeStruct((n, wi.shape[0]), F32) for wi in ws],
        grid=(n // tm,),
        in_specs=[pl.BlockSpec((tm, d), row), pl.BlockSpec((1, d), const)] + [_resident(wi) for wi in ws],
        out_specs=[pl.BlockSpec((tm, wi.shape[0]), row) for wi in ws],
        compiler_params=_params("parallel"),
        name="mix_ssm",
    )(x, g, *ws)


def _softmax_step_t(q_t, kb, vb_t, carry, mask):
    m, l, acc_t = carry
    s_t = _dot(kb, q_t)
    if mask is not None:
        s_t = jnp.where(mask, s_t, -jnp.inf)
    m_new = jnp.maximum(m, jnp.max(s_t, axis=0, keepdims=True))
    alpha = jnp.exp2(m - m_new)
    p_t = jnp.exp2(s_t - m_new)
    l = alpha * l + jnp.sum(p_t, axis=0, keepdims=True)
    acc_t = alpha * acc_t + _dot(vb_t, p_t.astype(BF16))
    return m_new, l, acc_t


def _attn_body(qt_ref, k_ref, vt_ref, o_ref, *, tq, tk, nh):
    i = pl.program_id(2)
    vd = vt_ref.shape[1]
    qts = [qt_ref[h] for h in range(nh)]

    def blocks(off, carries, mask):
        return tuple(_softmax_step_t(qts[h], k_ref[h, pl.ds(off, tk), :], vt_ref[h, :, pl.ds(off, tk)], carries[h], mask)
                     for h in range(nh))

    def full_block(j, carries):
        return blocks(pl.multiple_of(j * tk, tk), carries, None)

    carries = tuple((jnp.full((1, tq), -jnp.inf, F32), jnp.zeros((1, tq), F32), jnp.zeros((vd, tq), F32))
                    for _ in range(nh))
    carries = lax.fori_loop(0, i * (tq // tk), full_block, carries)
    keys = lax.broadcasted_iota(jnp.int32, (tk, tq), 0)
    queries = lax.broadcasted_iota(jnp.int32, (tk, tq), 1)
    for dblk in range(tq // tk):
        carries = blocks(pl.multiple_of(i * tq + dblk * tk, tk), carries, keys + dblk * tk <= queries)
    for h, (_, l, acc_t) in enumerate(carries):
        o_ref[:, h * vd:(h + 1) * vd] = (acc_t / l).T.astype(BF16)


def _attn_prompt(q3t, k3, v3t, batch):
    heads, dq, n = q3t.shape
    vd = v3t.shape[1]
    s = n // batch
    tq = min(ATTN_Q, s)
    tk = min(ATTN_K, tq)
    nq = s // tq
    nh = math.gcd(ATTN_HEADS, heads)
    return pl.pallas_call(
        functools.partial(_attn_body, tq=tq, tk=tk, nh=nh),
        out_shape=jax.ShapeDtypeStruct((n, heads * vd), BF16),
        grid=(batch, heads // nh, nq),
        in_specs=[
            pl.BlockSpec((nh, dq, tq), lambda b, h, i: (h, 0, b * nq + i)),
            pl.BlockSpec((nh, s, dq), lambda b, h, i: (h, b, 0)),
            pl.BlockSpec((nh, vd, s), lambda b, h, i: (h, 0, b)),
        ],
        out_specs=pl.BlockSpec((tq, nh * vd), lambda b, h, i: (b * nq + i, h)),
        compiler_params=_params("parallel", "parallel", "arbitrary"),
        name="attn_prompt",
    )(q3t, k3, v3t)


def _decode_body(pt_ref, ql_ref, qr_ref, cn_ref, kn_ref, ckv_hbm, krt_hbm, o_ref,
                 ckv_buf, krt_buf, sem, m_ref, l_ref, acc_ref, *, layer, pages, group, heads):
    b, p = pl.program_id(0), pl.program_id(1)
    nb, ns = pl.num_programs(0), pl.num_programs(1)
    ring = ckv_buf.shape[0]
    total = nb * ns
    t = b * ns + p
    slot = lax.rem(t, ring)
    page = ckv_buf.shape[1] // pages

    def step_copies(ahead, k):
        tn = t + ahead
        tn = jnp.where(tn >= total, tn - total, tn)
        pid = pt_ref[lax.div(tn, ns), lax.rem(tn, ns) * pages + k]
        sl = lax.rem(t + ahead, ring)
        rows = pl.ds(k * page, page)
        return (pltpu.make_async_copy(ckv_hbm.at[layer, pid], ckv_buf.at[sl, rows, :], sem.at[sl, 0]),
                pltpu.make_async_copy(krt_hbm.at[layer, pid], krt_buf.at[sl, :, rows], sem.at[sl, 1]))

    @pl.when(t == 0)
    def _():
        for ahead in range(ring - 1):
            for k in range(pages):
                for c in step_copies(ahead, k):
                    c.start()

    for k in range(pages):
        for c in step_copies(0, k):
            c.wait()

    @pl.when(p == 0)
    def _():
        m_ref[...] = jnp.full_like(m_ref, -jnp.inf)
        l_ref[...] = jnp.zeros_like(l_ref)
        acc_ref[...] = jnp.zeros_like(acc_ref)

    last_p = p == ns - 1
    ql = ql_ref[0]
    qr = qr_ref[0]

    chains = m_ref.shape[0]

    def update(c, s, kv):
        m = m_ref[c]
        m_new = jnp.maximum(m, jnp.max(s, axis=-1, keepdims=True))
        alpha = jnp.exp(m - m_new)
        pr = jnp.exp(s - m_new)
        l_ref[c] = alpha * l_ref[c] + jnp.sum(pr, axis=-1, keepdims=True)
        acc_ref[c] = alpha * acc_ref[c] + _dot(pr.astype(BF16), kv)
        m_ref[c] = m_new

    s_parts, kv_parts = [], []
    for j in range(pages // group):
        for k in range(j * group, (j + 1) * group):
            for c in step_copies(ring - 1, k):
                c.start()
        rows = pl.ds(j * group * page, group * page)
        kv_j = ckv_buf[slot, rows, :].astype(BF16)
        s_parts.append(lax.dot_general(ql, kv_j, _NT, preferred_element_type=F32)
                       + _dot(qr, krt_buf[slot, :, rows].astype(BF16)))
        kv_parts.append(kv_j)
    per = len(s_parts) // chains
    for c in range(chains):
        update(c, jnp.concatenate(s_parts[c * per:(c + 1) * per], axis=1),
               jnp.concatenate(kv_parts[c * per:(c + 1) * per], axis=0))

    @pl.when(last_p)
    def _():
        rows, nk = ql.shape[0], cn_ref.shape[1]
        qtok = lax.broadcasted_iota(jnp.int32, (rows, nk), 0) // heads
        ktok = lax.broadcasted_iota(jnp.int32, (rows, nk), 1)
        kvn = cn_ref[0]
        s = (lax.dot_general(ql, kvn, _NT, preferred_element_type=F32)
             + lax.dot_general(qr, kn_ref[0], _NT, preferred_element_type=F32))
        update(0, jnp.where(ktok <= qtok, s, -jnp.inf), kvn)
        m_all = m_ref[0]
        for c in range(1, chains):
            m_all = jnp.maximum(m_all, m_ref[c])
        l_all = jnp.zeros_like(m_all)
        acc_all = jnp.zeros(acc_ref.shape[1:], F32)
        for c in range(chains):
            w = jnp.exp(m_ref[c] - m_all)
            l_all = l_all + w * l_ref[c]
            acc_all = acc_all + w * acc_ref[c]
        o_ref[0] = (acc_all / l_all).astype(BF16)

    @pl.when(t == total - 1)
    def _():
        for ahead in range(1, ring):
            for k in range(pages):
                for c in step_copies(ahead, k):
                    c.wait()


def _attn_decode(layer, page_table, ql, qr, ckv_new, kr_new, cache_ckv, cache_krope_t, heads):
    bsz, rows, lat = ql.shape
    rope = qr.shape[2]
    n_pages = page_table.shape[1]
    page = cache_ckv.shape[2]
    g = math.gcd(DECODE_PAGES, n_pages)
    nk = ckv_new.shape[1]
    group = math.gcd(DECODE_GROUP, g)
    chains = math.gcd(DECODE_CHAINS, g // group)
    bmap = lambda b, p, pt: (b, 0, 0)
    grid_spec = pltpu.PrefetchScalarGridSpec(
        num_scalar_prefetch=1,
        grid=(bsz, n_pages // g),
        in_specs=[
            pl.BlockSpec((1, rows, lat), bmap),
            pl.BlockSpec((1, rows, rope), bmap),
            pl.BlockSpec((1, nk, lat), bmap),
            pl.BlockSpec((1, nk, rope), bmap),
            pl.BlockSpec(memory_space=pl.ANY),
            pl.BlockSpec(memory_space=pl.ANY),
        ],
        out_specs=pl.BlockSpec((1, rows, lat), bmap),
        scratch_shapes=[pltpu.VMEM((DECODE_SLOTS, g * page, lat), F32), pltpu.VMEM((DECODE_SLOTS, rope, g * page), F32),
                        pltpu.SemaphoreType.DMA((DECODE_SLOTS, 2)),
                        pltpu.VMEM((chains, rows, 1), F32), pltpu.VMEM((chains, rows, 1), F32),
                        pltpu.VMEM((chains, rows, lat), F32)],
    )
    return pl.pallas_call(
        functools.partial(_decode_body, layer=layer, pages=g, group=group, heads=heads),
        grid_spec=grid_spec,
        out_shape=jax.ShapeDtypeStruct((bsz, rows, lat), BF16),
        compiler_params=_params("arbitrary", "arbitrary"),
        name="attn_decode",
    )(page_table, ql, qr, ckv_new, kr_new, cache_ckv, cache_krope_t)


def _uv_body(ol_ref, wuv_ref, o_ref, *, heads):
    lat, vd = wuv_ref.shape[1], wuv_ref.shape[2]
    for hd in range(heads):
        o_ref[:, hd * vd:(hd + 1) * vd] = _dot(ol_ref[:, hd * lat:(hd + 1) * lat], wuv_ref[hd]).astype(BF16)


def _uv_proj(ol, wuv_h):
    n = ol.shape[0]
    heads, lat, vd = wuv_h.shape
    tm = min(ROW_TILE, n)
    return pl.pallas_call(
        functools.partial(_uv_body, heads=heads),
        out_shape=jax.ShapeDtypeStruct((n, heads * vd), BF16),
        grid=(n // tm,),
        in_specs=[pl.BlockSpec((tm, heads * lat), lambda i: (i, 0)), pl.BlockSpec(wuv_h.shape, lambda i: (0, 0, 0))],
        out_specs=pl.BlockSpec((tm, heads * vd), lambda i: (i, 0)),
        compiler_params=_params("parallel"),
        name="uv_proj",
    )(ol, wuv_h)


def _ssd_body(xbc_ref, z_ref, dt_ref, cprev_ref, sprev_ref, cw_ref, cb_ref, dtb_ref, alog_ref, dsk_ref, g_ref, r_ref,
              y_ref, cnew_ref, snew_ref, xp_ref, *, nseq, **dims):
    for sq in range(nseq):
        _ssd_seq(sq, xbc_ref, z_ref, dt_ref, cprev_ref, sprev_ref, cw_ref, cb_ref, dtb_ref, alog_ref, dsk_ref, g_ref,
                 r_ref, y_ref, cnew_ref, snew_ref, xp_ref, **dims)


def _ssd_seq(sq, xbc_ref, z_ref, dt_ref, cprev_ref, sprev_ref, cw_ref, cb_ref, dtb_ref, alog_ref, dsk_ref, g_ref, r_ref,
             y_ref, cnew_ref, snew_ref, xp_ref, *, chunk, l_real, heads, headdim, groups, d_state, single_chunk):
    c = pl.program_id(1)
    last = pl.num_programs(1) - 1
    d_inner = heads * headdim
    gw = d_inner // groups
    hpg = heads // groups
    state_in = sprev_ref if single_chunk else snew_ref

    @pl.when(c == 0)
    def _():
        if not single_chunk:
            snew_ref[sq] = sprev_ref[sq]
        xp_ref[sq, 0:CONV_PAD, :] = cprev_ref[sq]

    xp_ref[sq, CONV_PAD:CONV_PAD + chunk, :] = xbc_ref[sq]
    cw = cw_ref[...]
    y = cb_ref[...]
    for k in range(CONV_WIDTH):
        y = y + xp_ref[sq, CONV_PAD - (CONV_WIDTH - 1) + k:CONV_PAD - (CONV_WIDTH - 1) + k + chunk, :] * cw[k:k + 1, :]
    xbc = y * jax.nn.sigmoid(y)
    tail = xp_ref[sq, l_real:l_real + CONV_PAD, :]
    xp_ref[sq, 0:CONV_PAD, :] = tail

    @pl.when(c == last)
    def _():
        cnew_ref[sq] = tail

    xs = xbc[:, :d_inner]
    bm = xbc[:, d_inner:d_inner + groups * d_state].astype(BF16)
    cm = xbc[:, d_inner + groups * d_state:].astype(BF16)

    t_row = lax.broadcasted_iota(jnp.int32, (chunk, LANES), 0)
    dtr = dt_ref[sq] + dtb_ref[...]
    dt = jnp.maximum(dtr, 0.0) + jnp.log1p(jnp.exp(-jnp.abs(dtr)))
    if l_real < chunk:
        dt = jnp.where(t_row < l_real, dt, 0.0)
    a = dt * (-jnp.exp(alog_ref[...]))
    tri = (lax.broadcasted_iota(jnp.int32, (chunk, chunk), 0) >= lax.broadcasted_iota(jnp.int32, (chunk, chunk), 1))
    cs = _exact_dot(jnp.where(tri, 1.0, 0.0).astype(BF16), a)
    cs_last = cs[chunk - 1:chunk, :]
    ecs = jnp.exp(cs)
    dec = jnp.exp(cs_last - cs)
    expanded = _exact_dot(jnp.concatenate([dt, ecs, dec], axis=0), r_ref[...], split="lhs")
    dt_e = expanded[0:chunk]
    ecs_e = expanded[chunk:2 * chunk]
    dec_e = expanded[2 * chunk:3 * chunk]
    xdt = xs * dt_e

    eye = (lax.broadcasted_iota(jnp.int32, (LANES, LANES), 0) == lax.broadcasted_iota(jnp.int32, (LANES, LANES), 1))
    cs_t = _exact_dot(jnp.where(eye, 1.0, 0.0).astype(BF16), cs, dims=_NT)
    xdt_p = xdt.astype(BF16)
    bm_p = bm
    causal = lax.broadcasted_iota(jnp.int32, (chunk, chunk), 1) <= lax.broadcasted_iota(jnp.int32, (chunk, chunk), 0)
    ecl = jnp.exp(cs_last)

    y_groups = []
    for g in range(groups):
        cg = cm[:, g * d_state:(g + 1) * d_state]
        cb_mat = lax.dot_general(cg, bm_p[:, g * d_state:(g + 1) * d_state], _NT, preferred_element_type=F32)
        sg = state_in[sq, g * gw:(g + 1) * gw, :]
        y_off = lax.dot_general(cg, sg.astype(BF16), _NT, preferred_element_type=F32)
        y_heads = []
        for hh in range(hpg):
            hd = g * hpg + hh
            seg = jnp.exp(jnp.where(causal, cs[:, hd:hd + 1] - cs_t[hd:hd + 1, :], -jnp.inf))
            y_heads.append(_dot((cb_mat * seg).astype(BF16), xdt_p[:, hd * headdim:(hd + 1) * headdim]))
        y_groups.append(jnp.concatenate(y_heads, axis=1) + y_off * ecs_e[:, g * gw:(g + 1) * gw])
        xd = (xdt[:, g * gw:(g + 1) * gw] * dec_e[:, g * gw:(g + 1) * gw]).astype(BF16)
        upd = lax.dot_general(xd, bm[:, g * d_state:(g + 1) * d_state], _TN, preferred_element_type=F32)
        for hh in range(hpg):
            hd = g * hpg + hh
            r0 = g * gw + hh * headdim
            snew_ref[sq, r0:r0 + headdim, :] = (sg[hh * headdim:(hh + 1) * headdim, :] * ecl[0:1, hd:hd + 1]
                                                + upd[hh * headdim:(hh + 1) * headdim, :])

    yv = jnp.concatenate(y_groups, axis=1) + xs * dsk_ref[...]
    zv = z_ref[sq]
    yv = yv * (zv * jax.nn.sigmoid(zv))
    gv = g_ref[...]
    outs = [_rms(yv[:, g * gw:(g + 1) * gw], gv[:, g * gw:(g + 1) * gw]) for g in range(groups)]
    y_ref[sq] = jnp.concatenate(outs, axis=1).astype(BF16)


def _ssd(xbc, z, dt, conv_prev, state_prev, w, *, chunk, l_real):
    bsz, t, cd = xbc.shape
    heads, headdim, groups, d_state = w["ssm_heads"], w["ssm_headdim"], w["ssm_groups"], w["ssm_state"]
    d_inner = heads * headdim
    nc = t // chunk
    nseq = math.gcd(SSD_SEQS, bsz)
    tok = lambda b, c: (b, c, 0)
    seq = lambda b, c: (b, 0, 0)
    const = lambda b, c: (0, 0)
    consts = [w["conv_w"], w["conv_b"], w["dt_bias"], w["a_log"], w["d_skip"], w["g_ssm"], w["head_expand"]]
    body = functools.partial(_ssd_body, nseq=nseq, chunk=chunk, l_real=l_real, heads=heads, headdim=headdim,
                             groups=groups, d_state=d_state, single_chunk=nc == 1)
    return pl.pallas_call(
        body,
        out_shape=[jax.ShapeDtypeStruct((bsz, t, d_inner), BF16),
                   jax.ShapeDtypeStruct((bsz, CONV_PAD, cd), F32),
                   jax.ShapeDtypeStruct((bsz, d_inner, d_state), F32)],
        grid=(bsz // nseq, nc),
        in_specs=[
            pl.BlockSpec((nseq, chunk, cd), tok),
            pl.BlockSpec((nseq, chunk, d_inner), tok),
            pl.BlockSpec((nseq, chunk, LANES), tok),
            pl.BlockSpec((nseq, CONV_PAD, cd), seq),
            pl.BlockSpec((nseq, d_inner, d_state), seq),
        ] + [pl.BlockSpec(v.shape, const) for v in consts],
        out_specs=[pl.BlockSpec((nseq, chunk, d_inner), tok),
                   pl.BlockSpec((nseq, CONV_PAD, cd), seq),
                   pl.BlockSpec((nseq, d_inner, d_state), seq)],
        scratch_shapes=[pltpu.VMEM((nseq, CONV_PAD + chunk, cd), F32)],
        compiler_params=_params("parallel", "arbitrary"),
        name="ssd_mixer",
    )(xbc, z, dt, conv_prev, state_prev, *consts)


def _out_proj_body(x_ref, oa_ref, os_ref, wa_ref, ws_ref, o_ref):
    o_ref[...] = x_ref[...] + _dot(oa_ref[...], wa_ref[...]) + _dot(os_ref[...], ws_ref[...])


def _out_proj(x, oa, osm, wa, ws):
    n, d = x.shape
    tm = min(ROW_TILE, n)
    row = lambda i: (i, 0)
    const = lambda i: (0, 0)
    return pl.pallas_call(
        _out_proj_body,
        out_shape=jax.ShapeDtypeStruct((n, d), F32),
        grid=(n // tm,),
        in_specs=[pl.BlockSpec((tm, d), row), pl.BlockSpec((tm, oa.shape[1]), row), pl.BlockSpec((tm, osm.shape[1]), row),
                  pl.BlockSpec(wa.shape, const), pl.BlockSpec(ws.shape, const)],
        out_specs=pl.BlockSpec((tm, d), row),
        compiler_params=_params("parallel"),
        name="out_proj",
    )(x, oa, osm, wa, ws)


def _ple_body(x_ref, pe_ref, g_ref, wg_ref, wp_ref, gf_ref, o_ref, *, final):
    x = x_ref[...]
    gate = jax.nn.sigmoid(_dot(_rms(x, g_ref[...]).astype(BF16), wg_ref[...]))
    x = x + gate * _dot(pe_ref[...].astype(BF16), wp_ref[...])
    o_ref[...] = _rms(x, gf_ref[...]) if final else x


def _ple(x, pe, g, wg, wp, g_final, final):
    n, d = x.shape
    tm = min(ROW_TILE, n)
    row = lambda i: (i, 0)
    const = lambda i: (0, 0)
    return pl.pallas_call(
        functools.partial(_ple_body, final=final),
        out_shape=jax.ShapeDtypeStruct((n, d), F32),
        grid=(n // tm,),
        in_specs=[pl.BlockSpec((tm, d), row), pl.BlockSpec((tm, pe.shape[1]), row), pl.BlockSpec((1, d), const),
                  pl.BlockSpec(wg.shape, const), pl.BlockSpec(wp.shape, const), pl.BlockSpec((1, d), const)],
        out_specs=pl.BlockSpec((tm, d), row),
        compiler_params=_params("parallel"),
        name="ple_final",
    )(x, pe, g, wg, wp, g_final)


def _rope_tables(pos, rope):
    half = rope // 2
    inv = ROPE_THETA ** (-jnp.arange(half, dtype=F32) / half)
    ang = pos.astype(F32)[:, None] * inv[None, :]
    cos, sin = jnp.cos(ang), jnp.sin(ang)
    return jnp.concatenate([cos, cos], axis=-1), jnp.concatenate([sin, sin], axis=-1)


def _rot_rows(w_t, rope):
    d = w_t.shape[1]
    wr = w_t.reshape(-1, 2, rope // 2, d)
    return jnp.stack([-wr[:, 1], wr[:, 0]], axis=1).reshape(-1, d)


def _prep_layer(i, p):
    d = p["w_in"].shape[1]
    lat, heads, nope = p["w_uk"].shape[1:]
    vd = p["w_uv"].shape[3]
    rope = p["cache_krope"].shape[3]
    ssm_heads, headdim, d_state = p["state_ssm"].shape[2:]
    d_inner = ssm_heads * headdim
    conv_dim = p["conv_w"].shape[2]
    groups = (conv_dim - d_inner) // (2 * d_state)
    q_dim = heads * (nope + rope)
    w_in_t = jnp.swapaxes(p["w_in"], 1, 2)[i]
    o = 0
    wq_t = w_in_t[o:o + q_dim].reshape(heads, nope + rope, d); o += q_dim
    w_ckv_t = w_in_t[o:o + lat]; o += lat
    w_kr_t = w_in_t[o:o + rope]; o += rope
    w_z_t = w_in_t[o:o + d_inner]; o += d_inner
    w_xbc_t = w_in_t[o:o + conv_dim]; o += conv_dim
    w_dt_t = w_in_t[o:o + ssm_heads]
    w_qn_t = wq_t[:, :nope].reshape(heads * nope, d)
    w_qr_t = wq_t[:, nope:].reshape(heads * rope, d)
    b = lambda t: t.astype(BF16)
    pad_heads = lambda v: jnp.pad(v.reshape(1, -1), ((0, 0), (0, LANES - ssm_heads)))
    head_of_channel = jnp.arange(d_inner) // headdim
    w_out = p["w_out"][i]
    return dict(
        heads=heads, nope=nope, rope=rope, lat=lat,
        ssm_heads=ssm_heads, ssm_headdim=headdim, ssm_groups=groups, ssm_state=d_state,
        g_ffn1=p["g_ffn1"][i][None], w1_a=p["w1_a"][i], w3_a=p["w3_a"][i], w2_a=p["w2_a"][i],
        g_ffn2=p["g_ffn2"][i][None], w1_b=p["w1_b"][i], w3_b=p["w3_b"][i], w2_b=p["w2_b"][i],
        g_mix=p["g_mix"][i][None],
        w_qn_t=b(w_qn_t), w_qr_t=b(w_qr_t), w_qr_rot_t=b(_rot_rows(w_qr_t, rope)),
        w_ckv_t=b(w_ckv_t), w_kr_t=b(w_kr_t), w_kr_rot_t=b(_rot_rows(w_kr_t, rope)), g_kv=p["g_kv"][i][None],
        w_uk=b(p["w_uk"][i].reshape(lat, heads * nope)), w_uv_t=b(p["w_uv"][i].reshape(lat, heads * vd).T),
        w_uk_t=b(jnp.transpose(p["w_uk"][i], (1, 2, 0))), w_uv_h=b(jnp.transpose(p["w_uv"][i], (1, 0, 2))),
        w_z_t=b(w_z_t), w_xbc_t=b(w_xbc_t), w_dt_t=b(jnp.pad(w_dt_t, ((0, LANES - ssm_heads), (0, 0)))),
        conv_w=p["conv_w"][i], conv_b=p["conv_b"][i][None], dt_bias=pad_heads(p["dt_bias"][i]),
        a_log=pad_heads(p["a_log"][i]), d_skip=jnp.repeat(p["d_skip"][i], headdim)[None], g_ssm=p["g_ssm"][i][None],
        head_expand=(jnp.arange(LANES)[:, None] == head_of_channel[None, :]).astype(BF16),
        w_out_a=b(w_out[:heads * vd]), w_out_s=b(w_out[heads * vd:]),
        g_ple=p["g_ple"][i][None], w_ple_gate=b(p["w_ple_gate"][i]), w_ple_proj=b(p["w_ple_proj"][i]),
    )


def _trunk_layer(x, pe, cos, sin, conv_prev, state_prev, attn, w, g_final, *, bsz, chunk, l_real, final):
    n, d = x.shape
    t = n // bsz
    scale = (w["nope"] + w["rope"]) ** -0.5
    x = _ffn(x, w["g_ffn1"], w["w1_a"], w["w3_a"], w["w2_a"])
    o_att, ckv, kr = attn(x, cos, sin, scale)
    z, xbc, dt = _mix_ssm(x, w["g_mix"], w)
    seq = lambda v: v.reshape(bsz, t, v.shape[1])
    if t % chunk:
        pad = lambda v: jnp.pad(seq(v), ((0, 0), (0, chunk - t), (0, 0)))
    else:
        pad = seq
    o_ssm, conv_new, ssm_new = _ssd(pad(xbc), pad(z), pad(dt), conv_prev, state_prev, w, chunk=chunk, l_real=l_real)
    o_ssm = o_ssm[:, :t].reshape(n, -1)
    x = _out_proj(x, o_att, o_ssm, w["w_out_a"], w["w_out_s"])
    x = _ffn(x, w["g_ffn2"], w["w1_b"], w["w3_b"], w["w2_b"])
    x = _ple(x, pe, w["g_ple"], w["w_ple_gate"], w["w_ple_proj"], g_final, final)
    return x, ckv, kr, conv_new[:, CONV_PAD - (CONV_WIDTH - 1):], ssm_new


def kernel(x_prompt, x_sample, p_prompt, p_sample, cache_ckv, cache_krope, state_conv, state_ssm, page_table, g_ffn1, w1_a, w3_a, w2_a, g_mix, w_in, g_kv, w_uk, w_uv, conv_w, conv_b, dt_bias, a_log, d_skip, g_ssm, w_out, g_ffn2, w1_b, w3_b, w2_b, g_ple, w_ple_gate, w_ple_proj, g_final):
    params = dict(cache_krope=cache_krope, state_ssm=state_ssm, g_ffn1=g_ffn1, w1_a=w1_a, w3_a=w3_a, w2_a=w2_a,
                  g_mix=g_mix, w_in=w_in, g_kv=g_kv, w_uk=w_uk, w_uv=w_uv, conv_w=conv_w, conv_b=conv_b,
                  dt_bias=dt_bias, a_log=a_log, d_skip=d_skip, g_ssm=g_ssm, w_out=w_out, g_ffn2=g_ffn2, w1_b=w1_b,
                  w3_b=w3_b, w2_b=w2_b, g_ple=g_ple, w_ple_gate=w_ple_gate, w_ple_proj=w_ple_proj)
    depth = w_in.shape[0]
    b_p, s_p, d = x_prompt.shape
    b_s, t_s, _ = x_sample.shape
    heads, rope = w_uk.shape[2], cache_krope.shape[3]
    lat = w_uk.shape[1]
    ssm_heads, headdim, d_state = state_ssm.shape[2:]
    d_inner = ssm_heads * headdim
    conv_dim = conv_w.shape[2]
    past_len = page_table.shape[1] * cache_ckv.shape[2]
    cos_p, sin_p = _rope_tables(jnp.arange(s_p), rope)
    cos_s, sin_s = (jnp.tile(t, (b_s, heads)) for t in _rope_tables(past_len + jnp.arange(t_s), rope))
    chunk_p = math.gcd(s_p, SSD_CHUNK)
    chunk_s = -(-t_s // BF16_ROWS) * BF16_ROWS
    g_fin = g_final[None]
    cache_krope_t = jnp.swapaxes(cache_krope, 2, 3)
    hp = x_prompt.reshape(b_p * s_p, d)
    hs = x_sample.reshape(b_s * t_s, d)
    outs = [[] for _ in range(8)]
    for i in range(depth):
        w = _prep_layer(i, params)
        final = i == depth - 1

        def attn_p(x, cos, sin, scale):
            q3t, k3, v3t, ckv, kr = _mix_attn_prompt(x, w["g_mix"], cos, sin, w, scale * LOG2_E)
            return _attn_prompt(q3t, k3, v3t, b_p), ckv, kr

        def attn_s(x, cos, sin, scale):
            ql, qr, ckv, kr = _mix_attn_sample(x, w["g_mix"], cos, sin, w, scale)
            new = lambda v: jnp.pad(v.reshape(b_s, t_s, -1), ((0, 0), (0, chunk_s - t_s), (0, 0))).astype(BF16)
            ol = _attn_decode(i, page_table, ql.reshape(b_s, t_s * heads, lat), qr.reshape(b_s, t_s * heads, rope),
                              new(ckv), new(kr), cache_ckv, cache_krope_t, heads)
            return _uv_proj(ol.reshape(b_s * t_s, heads * lat), w["w_uv_h"]), ckv, kr

        conv0 = jnp.zeros((b_p, CONV_PAD, conv_dim), F32)
        ssm0 = jnp.zeros((b_p, d_inner, d_state), F32)
        hp, c1, k1, cv1, s1 = _trunk_layer(hp, p_prompt[i].reshape(b_p * s_p, -1), cos_p, sin_p, conv0, ssm0, attn_p, w,
                                           g_fin, bsz=b_p, chunk=chunk_p, l_real=chunk_p, final=final)
        conv_s = jnp.pad(state_conv[i], ((0, 0), (CONV_PAD - (CONV_WIDTH - 1), 0), (0, 0)))
        hs, c2, k2, cv2, s2 = _trunk_layer(hs, p_sample[i].reshape(b_s * t_s, -1), cos_s, sin_s, conv_s,
                                           state_ssm[i].reshape(b_s, d_inner, d_state), attn_s, w, g_fin,
                                           bsz=b_s, chunk=chunk_s, l_real=t_s, final=final)
        for lst, v in zip(outs, (c1.reshape(b_p, s_p, -1), k1.reshape(b_p, s_p, -1), cv1,
                                 s1.reshape(b_p, ssm_heads, headdim, d_state),
                                 c2.reshape(b_s, t_s, -1), k2.reshape(b_s, t_s, -1), cv2,
                                 s2.reshape(b_s, ssm_heads, headdim, d_state))):
            lst.append(v)
    return (hp.reshape(b_p, s_p, d), hs.reshape(b_s, t_s, d)) + tuple(jnp.stack(o) for o in outs)
```
